```python
import jax, jax.numpy as jnp
from jax import lax
import numpy as np

D_MODEL = 2048
BATCH = 1
SEQ = 16384
DEPTH = 4
DEC_BATCH = 8
DEC_SEQ = 4096
PAST_LEN = 128

BRANCH_WIDTH = D_MODEL // 2
N_BRANCHES = 3
MLA_HEADS = 8
QK_NOPE_DIM = 128
QK_ROPE_DIM = 64
V_HEAD_DIM = 128
Q_LORA_RANK = 768
KV_LORA_RANK = 512
ROPE_THETA = 10000.0
Q_BLOCK = 128
RWKV_HEAD_SIZE = 64
RWKV_HEADS = BRANCH_WIDTH // RWKV_HEAD_SIZE
DECAY_LORA = 64
AAA_LORA = 64
GATE_LORA = 160
CONV_WIDTH = 3
D_FF = ((8 * D_MODEL + 3 * 256 - 1) // (3 * 256)) * 256
LN_EPS = 1e-5
RMS_EPS = 1e-6
LNX_EPS = 1e-5 * RWKV_HEAD_SIZE
DEEPNORM_ALPHA = (2 * DEPTH) ** 0.25
DEEPNORM_BETA = (8 * DEPTH) ** -0.25
MLA_COLS = Q_LORA_RANK + KV_LORA_RANK + QK_ROPE_DIM
RWKV_COLS = 3 * BRANCH_WIDTH + 2 * DECAY_LORA + 2 * AAA_LORA + GATE_LORA
CONV_COLS = 3 * BRANCH_WIDTH
GATE_COLS = N_BRANCHES * D_MODEL
IN_COLS = MLA_COLS + RWKV_COLS + CONV_COLS + GATE_COLS

kernel_name = 'hybrid_mla_rwkv7_shortconv_encoder'


def layer_norm(x, g, b, eps=LN_EPS):
    xf = x.astype(jnp.float32)
    mu = jnp.mean(xf, axis=-1, keepdims=True)
    var = jnp.mean(jnp.square(xf - mu), axis=-1, keepdims=True)
    y = (xf - mu) * lax.rsqrt(var + eps) * g.astype(jnp.float32) + b.astype(jnp.float32)
    return y.astype(x.dtype)


def rms_norm(x, g, eps=RMS_EPS):
    xf = x.astype(jnp.float32)
    y = xf * lax.rsqrt(jnp.mean(jnp.square(xf), axis=-1, keepdims=True) + eps) * g.astype(jnp.float32)
    return y.astype(x.dtype)


def rope_tables(seq, dtype):
    pos = jnp.arange(seq, dtype=jnp.float32)
    inv = ROPE_THETA ** (-jnp.arange(0, QK_ROPE_DIM, 2, dtype=jnp.float32) / QK_ROPE_DIM)
    ang = pos[:, None] * inv[None, :]
    return jnp.cos(ang).astype(dtype), jnp.sin(ang).astype(dtype)


def apply_rope(x, cos, sin):
    x1, x2 = jnp.split(x, 2, axis=-1)
    return jnp.concatenate([x1 * cos - x2 * sin, x1 * sin + x2 * cos], axis=-1)


def mla_branch(u, cos, sin, q_norm_g, w_uq, kv_norm_g, w_ukv):
    bsz, seq, _ = u.shape
    cq, ckv, kr = jnp.split(u, [Q_LORA_RANK, Q_LORA_RANK + KV_LORA_RANK], axis=-1)
    q = (rms_norm(cq, q_norm_g) @ w_uq).reshape(bsz, seq, MLA_HEADS, QK_NOPE_DIM + QK_ROPE_DIM)
    qn, qr = q[..., :QK_NOPE_DIM], q[..., QK_NOPE_DIM:]
    kv = (rms_norm(ckv, kv_norm_g) @ w_ukv).reshape(bsz, seq, MLA_HEADS, QK_NOPE_DIM + V_HEAD_DIM)
    kn, v = kv[..., :QK_NOPE_DIM], kv[..., QK_NOPE_DIM:]
    qr = apply_rope(qr, cos[None, :, None, :], sin[None, :, None, :])
    kr = apply_rope(kr, cos[None], sin[None])
    scale = (QK_NOPE_DIM + QK_ROPE_DIM) ** -0.5
    nblk = seq // Q_BLOCK

    def to_blocks(t):
        return t.reshape(bsz, nblk, Q_BLOCK, MLA_HEADS, t.shape[-1]).swapaxes(0, 1)

    def attend(blk):
        qn_b, qr_b = blk
        s = jnp.einsum('bqhd,bkhd->bhqk', qn_b, kn) + jnp.einsum('bqhr,bkr->bhqk', qr_b, kr)
        p = jax.nn.softmax(s.astype(jnp.float32) * scale, axis=-1)
        return jnp.einsum('bhqk,bkhd->bqhd', p.astype(v.dtype), v)

    o = lax.map(attend, (to_blocks(qn), to_blocks(qr)))
    return o.swapaxes(0, 1).reshape(bsz, seq, MLA_HEADS * V_HEAD_DIM)


def token_shift(u, mu):
    prev = jnp.pad(u, ((0, 0), (1, 0), (0, 0)))[:, :-1]
    nxt = jnp.pad(u, ((0, 0), (0, 1), (0, 0)))[:, 1:]
    return u + mu[0] * (prev - u) + mu[1] * (nxt - u)


def wkv7_scan(r, decay, k, v, a_vec, b_vec, reverse):
    bsz = r.shape[0]
    xs = tuple(t.astype(jnp.float32).swapaxes(0, 1) for t in (r, decay, k, v, a_vec, b_vec))
    state0 = jnp.zeros((bsz, RWKV_HEADS, RWKV_HEAD_SIZE, RWKV_HEAD_SIZE), jnp.float32)

    def step(state, inp):
        r_t, w_t, k_t, v_t, a_t, b_t = inp
        sa = jnp.einsum('bhvk,bhk->bhv', state, a_t)
        state = (state * w_t[:, :, None, :] + sa[..., None] * b_t[:, :, None, :]
                 + v_t[..., None] * k_t[:, :, None, :])
        return state, jnp.einsum('bhvk,bhk->bhv', state, r_t)

    _, ys = lax.scan(step, state0, xs, reverse=reverse)
    return ys.swapaxes(0, 1)


def rwkv_branch(u, mu, w0, w_up, a0, a_up, g_up, k_k, k_a, r_k, lnx_g, lnx_b):
    bsz, seq, _ = u.shape
    C = BRANCH_WIDTH
    u = token_shift(u, mu)
    r, k, v, wd, ad, gd = jnp.split(
        u, [C, 2 * C, 3 * C, 3 * C + 2 * DECAY_LORA, 3 * C + 2 * DECAY_LORA + 2 * AAA_LORA], axis=-1)

    def heads(t):
        return t.reshape(bsz, seq, RWKV_HEADS, RWKV_HEAD_SIZE)

    g = jax.nn.sigmoid(gd) @ g_up
    w_lora = jnp.einsum('bsdl,dlc->bsdc', jnp.tanh(wd.reshape(bsz, seq, 2, DECAY_LORA)), w_up)
    logw = -jax.nn.softplus(-(w0 + w_lora)) - 0.5
    decay = jnp.exp(-jnp.exp(logw.astype(jnp.float32)))
    a = jax.nn.sigmoid(a0 + jnp.einsum('bsdl,dlc->bsdc', ad.reshape(bsz, seq, 2, AAA_LORA), a_up))
    kk = heads(k * k_k).astype(jnp.float32)
    kk = kk / jnp.maximum(jnp.sqrt(jnp.sum(jnp.square(kk), axis=-1, keepdims=True)), 1e-12)
    kk = kk.astype(u.dtype)
    r_h, v_h = heads(r), heads(v)
    ys, bonuses = [], []
    for d in range(2):
        a_d = a[:, :, d]
        k_d = heads(k * (1 + (a_d - 1) * k_a))
        ys.append(wkv7_scan(r_h, heads(decay[:, :, d]), k_d, v_h, -kk, kk * heads(a_d), reverse=(d == 1)))
        bonuses.append(jnp.sum(r_h * k_d * r_k, axis=-1, keepdims=True) * v_h)
    y = ys[0] + ys[1]
    mu_y = jnp.mean(y, axis=-1, keepdims=True)
    var_y = jnp.mean(jnp.square(y - mu_y), axis=-1, keepdims=True)
    y = ((y - mu_y) * lax.rsqrt(var_y + LNX_EPS)).reshape(bsz, seq, C)
    y = (y * lnx_g.astype(jnp.float32) + lnx_b.astype(jnp.float32)).astype(u.dtype)
    return (y + (bonuses[0] + bonuses[1]).reshape(bsz, seq, C)) * g


def conv_branch(u, conv_w):
    bg, cg, h = jnp.split(u, 3, axis=-1)
    hp = jnp.pad(cg * h, ((0, 0), (1, 1), (0, 0)))
    conv = conv_w[0] * hp[:, :-2] + conv_w[1] * hp[:, 1:-1] + conv_w[2] * hp[:, 2:]
    return bg * conv


def encoder_layer(x, cos, sin, w_in, q_norm_g, w_uq, kv_norm_g, w_ukv,
                  rwkv_mu, rwkv_w0, rwkv_w_up, rwkv_a0, rwkv_a_up, rwkv_g_up,
                  rwkv_k_k, rwkv_k_a, rwkv_r_k, rwkv_lnx_g, rwkv_lnx_b,
                  conv_w, w_branch, w_out, ln1_g, ln1_b, w_ffn_in, w_ffn_out, ln2_g, ln2_b):
    bsz, seq, _ = x.shape
    proj = x @ w_in
    u_a, u_b, u_c, u_g = jnp.split(
        proj, [MLA_COLS, MLA_COLS + RWKV_COLS, MLA_COLS + RWKV_COLS + CONV_COLS], axis=-1)
    y_a = mla_branch(u_a, cos, sin, q_norm_g, w_uq, kv_norm_g, w_ukv)
    y_b = rwkv_branch(u_b, rwkv_mu, rwkv_w0, rwkv_w_up, rwkv_a0, rwkv_a_up, rwkv_g_up,
                      rwkv_k_k, rwkv_k_a, rwkv_r_k, rwkv_lnx_g, rwkv_lnx_b)
    y_c = conv_branch(u_c, conv_w)
    branches = jnp.stack([y_a, y_b, y_c], axis=2)
    gates = jax.nn.sigmoid(u_g.reshape(bsz, seq, N_BRANCHES, D_MODEL))
    merged = jnp.sum(gates * jnp.einsum('bsiw,iwd->bsid', branches, w_branch), axis=2)
    h = layer_norm(DEEPNORM_ALPHA * x + merged @ w_out, ln1_g, ln1_b)
    gate, up = jnp.split(h @ w_ffn_in, 2, axis=-1)
    f = (jax.nn.silu(gate) * up) @ w_ffn_out
    return layer_norm(DEEPNORM_ALPHA * h + f, ln2_g, ln2_b)


def run_trunk(x, ln_in_g, ln_in_b, layer_weights):
    cos, sin = rope_tables(x.shape[1], x.dtype)
    x = layer_norm(x, ln_in_g, ln_in_b)
    for l in range(DEPTH):
        x = encoder_layer(x, cos, sin, *(w[l] for w in layer_weights))
    return x


def setup_inputs(seed: int = 0) -> dict:
    key = jax.random.key(seed)
    ks = jax.random.split(key, 32)
    L, C = DEPTH, BRANCH_WIDTH

    def nrm(k, shape, scale):
        return scale * jax.random.normal(k, shape, jnp.float32)

    return {
        'x_prompt': nrm(ks[0], (BATCH, SEQ, D_MODEL), 1.0),
        'x_sample': nrm(ks[1], (DEC_BATCH, DEC_SEQ, D_MODEL), 1.0),
        'ln_in_g': 1.0 + nrm(ks[2], (D_MODEL,), 0.02),
        'ln_in_b': nrm(ks[3], (D_MODEL,), 0.02),
        'w_in': nrm(ks[4], (L, D_MODEL, IN_COLS), D_MODEL ** -0.5),
        'q_norm_g': 1.0 + nrm(ks[5], (L, Q_LORA_RANK), 0.02),
        'w_uq': nrm(ks[6], (L, Q_LORA_RANK, MLA_HEADS * (QK_NOPE_DIM + QK_ROPE_DIM)), Q_LORA_RANK ** -0.5),
        'kv_norm_g': 1.0 + nrm(ks[7], (L, KV_LORA_RANK), 0.02),
        'w_ukv': nrm(ks[8], (L, KV_LORA_RANK, MLA_HEADS * (QK_NOPE_DIM + V_HEAD_DIM)), KV_LORA_RANK ** -0.5),
        'rwkv_mu': jax.random.uniform(ks[9], (L, 2, RWKV_COLS), jnp.float32, 0.0, 0.5),
        'rwkv_w0': jax.random.uniform(ks[10], (L, 2, C), jnp.float32, -6.0, -1.0),
        'rwkv_w_up': nrm(ks[11], (L, 2, DECAY_LORA, C), DECAY_LORA ** -0.5),
        'rwkv_a0': nrm(ks[12], (L, 2, C), 0.1),
        'rwkv_a_up': nrm(ks[13], (L, 2, AAA_LORA, C), AAA_LORA ** -0.5),
        'rwkv_g_up': nrm(ks[14], (L, GATE_LORA, C), GATE_LORA ** -0.5),
        'rwkv_k_k': 0.85 + nrm(ks[15], (L, C), 0.02),
        'rwkv_k_a': 1.0 + nrm(ks[16], (L, C), 0.02),
        'rwkv_r_k': nrm(ks[17], (L, RWKV_HEADS, RWKV_HEAD_SIZE), 0.1),
        'rwkv_lnx_g': 1.0 + nrm(ks[18], (L, C), 0.02),
        'rwkv_lnx_b': nrm(ks[19], (L, C), 0.02),
        'conv_w': nrm(ks[20], (L, CONV_WIDTH, C), CONV_WIDTH ** -0.5),
        'w_branch': nrm(ks[21], (L, N_BRANCHES, C, D_MODEL), DEEPNORM_BETA * C ** -0.5),
        'w_out': nrm(ks[22], (L, D_MODEL, D_MODEL), DEEPNORM_BETA * D_MODEL ** -0.5),
        'ln1_g': 1.0 + nrm(ks[23], (L, D_MODEL), 0.02),
        'ln1_b': nrm(ks[24], (L, D_MODEL), 0.02),
        'w_ffn_in': nrm(ks[25], (L, D_MODEL, 2 * D_FF), D_MODEL ** -0.5),
        'w_ffn_out': nrm(ks[26], (L, D_FF, D_MODEL), DEEPNORM_BETA * D_FF ** -0.5),
        'ln2_g': 1.0 + nrm(ks[27], (L, D_MODEL), 0.02),
        'ln2_b': nrm(ks[28], (L, D_MODEL), 0.02),
    }


def reference(x_prompt, x_sample, ln_in_g, ln_in_b, w_in, q_norm_g, w_uq, kv_norm_g, w_ukv,
              rwkv_mu, rwkv_w0, rwkv_w_up, rwkv_a0, rwkv_a_up, rwkv_g_up, rwkv_k_k, rwkv_k_a,
              rwkv_r_k, rwkv_lnx_g, rwkv_lnx_b, conv_w, w_branch, w_out, ln1_g, ln1_b,
              w_ffn_in, w_ffn_out, ln2_g, ln2_b):
    layer_weights = (w_in, q_norm_g, w_uq, kv_norm_g, w_ukv,
                     rwkv_mu, rwkv_w0, rwkv_w_up, rwkv_a0, rwkv_a_up, rwkv_g_up,
                     rwkv_k_k, rwkv_k_a, rwkv_r_k, rwkv_lnx_g, rwkv_lnx_b,
                     conv_w, w_branch, w_out, ln1_g, ln1_b, w_ffn_in, w_ffn_out, ln2_g, ln2_b)
    y_prompt = run_trunk(x_prompt, ln_in_g, ln_in_b, layer_weights)
    y_sample = run_trunk(x_sample, ln_in_g, ln_in_b, layer_weights)
    return (y_prompt, y_sample)
```

```python
import functools

import jax
import jax.numpy as jnp
from jax import lax
from jax.experimental import pallas as pl
from jax.experimental.pallas import tpu as pltpu

F32 = jnp.float32
BF16 = jnp.bfloat16

D_MODEL = 2048
DEPTH = 4
WIDTH = D_MODEL // 2
MLA_HEADS = 8
NOPE = 128
ROPE = 64
VDIM = 128
Q_LORA = 768
KV_LORA = 512
ROPE_THETA = 10000.0
RW_HEADS = 16
RW_N = 64
DECAY_LORA = 64
AAA_LORA = 64
GATE_LORA = 160
D_FF = 5632
LN_EPS = 1e-5
RMS_EPS = 1e-6
LNX_EPS = 1e-5 * RW_N
ALPHA = (2 * DEPTH) ** 0.25
MLA_COLS = Q_LORA + KV_LORA + ROPE
RWKV_COLS = 3 * WIDTH + 2 * DECAY_LORA + 2 * AAA_LORA + GATE_LORA
CONV_COLS = 3 * WIDTH
GATE_COLS = 3 * D_MODEL
MLA_PAD = 1536
RWKV_PAD = 3584
GD_PAD = 256
QK_PAD = 256
CHUNK = 64

VMEM_LIMIT = 56 * 1024 * 1024


def _cparams(sem):
    return pltpu.CompilerParams(dimension_semantics=sem, vmem_limit_bytes=VMEM_LIMIT)


def _dot(a, b):
    return jnp.dot(a, b, preferred_element_type=F32)


def _dot_nt(a, b):
    return lax.dot_general(a, b, (((1,), (1,)), ((), ())), preferred_element_type=F32)


def _dot_tn(a, b):
    return lax.dot_general(a, b, (((0,), (0,)), ((), ())), preferred_element_type=F32)


def _split2(x):
    hi = x.astype(BF16)
    lo = (x - hi.astype(F32)).astype(BF16)
    return hi, lo


def _split3(x):
    hi = x.astype(BF16)
    r1 = x - hi.astype(F32)
    mid = r1.astype(BF16)
    lo = (r1 - mid.astype(F32)).astype(BF16)
    return hi, mid, lo


def _seg_sum(x, ones_bd):
    hi, lo = _split2(x)
    return _dot(hi, ones_bd) + _dot(lo, ones_bd)


def _pick(n, target):
    b = min(n, target)
    while n % b:
        b //= 2
    return b


def _ln_math(x, g, b):
    mu = jnp.mean(x, axis=-1, keepdims=True)
    xc = x - mu
    var = jnp.mean(xc * xc, axis=-1, keepdims=True)
    return xc * lax.rsqrt(var + LN_EPS) * g + b


def _ln_kernel(x_ref, g_ref, b_ref, o_ref, ob_ref):
    y = _ln_math(x_ref[...], g_ref[...], b_ref[...])
    o_ref[...] = y
    ob_ref[...] = y.astype(BF16)


def layer_norm_in(x, g, b):
    m, d = x.shape
    bm = _pick(m, 512)
    return pl.pallas_call(
        _ln_kernel,
        grid=(m // bm,),
        in_specs=[pl.BlockSpec((bm, d), lambda i: (i, 0)),
                  pl.BlockSpec((1, d), lambda i: (0, 0)),
                  pl.BlockSpec((1, d), lambda i: (0, 0))],
        out_specs=[pl.BlockSpec((bm, d), lambda i: (i, 0)),
                   pl.BlockSpec((bm, d), lambda i: (i, 0))],
        out_shape=[jax.ShapeDtypeStruct((m, d), F32), jax.ShapeDtypeStruct((m, d), BF16)],
        compiler_params=_cparams(("parallel",)),
        name="ln_in",
    )(x, g, b)


def _mm_kernel(x_ref, w_ref, o_ref):
    o_ref[...] = _dot(x_ref[...], w_ref[...]).astype(o_ref.dtype)


def matmul(x, w, bn, out_dtype=F32, name="mm"):
    m, k = x.shape
    n = w.shape[1]
    bm = _pick(m, 1024)
    return pl.pallas_call(
        _mm_kernel,
        grid=(m // bm, n // bn),
        in_specs=[pl.BlockSpec((bm, k), lambda i, j: (i, 0)),
                  pl.BlockSpec((k, bn), lambda i, j: (0, j))],
        out_specs=pl.BlockSpec((bm, bn), lambda i, j: (i, j)),
        out_shape=jax.ShapeDtypeStruct((m, n), out_dtype),
        compiler_params=_cparams(("parallel", "arbitrary")),
        name=name,
    )(x, w)


def _rms(x, g):
    return x * lax.rsqrt(jnp.mean(x * x, axis=-1, keepdims=True) + RMS_EPS) * g


def _mla_prep_kernel(u_ref, qg_ref, kvg_ref, wq_ref, wkv_ref, cos_ref, sin_ref,
                     q_ref, k_ref, v_ref):
    scale = (NOPE + ROPE) ** -0.5
    cz = cos_ref[...]
    sz = sin_ref[...]
    cq = _rms(u_ref[:, :Q_LORA], qg_ref[...]).astype(BF16)
    pq = _dot(cq, wq_ref[...])
    ckv = _rms(u_ref[:, Q_LORA:Q_LORA + KV_LORA], kvg_ref[...]).astype(BF16)
    pkv = _dot(ckv, wkv_ref[...])
    o = Q_LORA + KV_LORA
    krz = (u_ref[:, o:o + 128] * cz + u_ref[:, o + 128:o + 256] * sz).astype(BF16)
    for h in range(MLA_HEADS):
        a, b = h * 128, (h + 1) * 128
        q_ref[:, h * QK_PAD:h * QK_PAD + 128] = (pq[:, a:b] * scale).astype(BF16)
        q_ref[:, h * QK_PAD + 128:(h + 1) * QK_PAD] = (
            (pq[:, WIDTH + a:WIDTH + b] * cz + pq[:, 2 * WIDTH + a:2 * WIDTH + b] * sz) * scale
        ).astype(BF16)
        k_ref[:, h * QK_PAD:h * QK_PAD + 128] = pkv[:, a:b].astype(BF16)
        k_ref[:, h * QK_PAD + 128:(h + 1) * QK_PAD] = krz
    v_ref[...] = pkv[:, WIDTH:].astype(BF16)


def mla_prep(u_a, qg, kvg, wq, wkv, cosz, sinz, seq):
    m = u_a.shape[0]
    bm = _pick(seq, 512)
    nb = seq // bm
    full = lambda i: (0, 0)
    return pl.pallas_call(
        _mla_prep_kernel,
        grid=(m // bm,),
        in_specs=[pl.BlockSpec((bm, MLA_PAD), lambda i: (i, 0)),
                  pl.BlockSpec((1, Q_LORA), full),
                  pl.BlockSpec((1, KV_LORA), full),
                  pl.BlockSpec(wq.shape, full),
                  pl.BlockSpec(wkv.shape, full),
                  pl.BlockSpec((bm, 128), lambda i: (i % nb, 0)),
                  pl.BlockSpec((bm, 128), lambda i: (i % nb, 0))],
        out_specs=[pl.BlockSpec((bm, MLA_HEADS * QK_PAD), lambda i: (i, 0)),
                   pl.BlockSpec((bm, MLA_HEADS * QK_PAD), lambda i: (i, 0)),
                   pl.BlockSpec((bm, WIDTH), lambda i: (i, 0))],
        out_shape=[jax.ShapeDtypeStruct((m, MLA_HEADS * QK_PAD), BF16),
                   jax.ShapeDtypeStruct((m, MLA_HEADS * QK_PAD), BF16),
                   jax.ShapeDtypeStruct((m, WIDTH), BF16)],
        compiler_params=_cparams(("parallel",)),
        name="mla_prep",
    )(u_a, qg, kvg, wq, wkv, cosz, sinz)


def _attn_kernel(q_ref, k_ref, v_ref, o_ref, m_ref, l_ref, acc_ref):
    j = pl.program_id(3)

    @pl.when(j == 0)
    def _():
        m_ref[...] = jnp.full(m_ref.shape, -jnp.inf, F32)
        l_ref[...] = jnp.zeros(l_ref.shape, F32)
        acc_ref[...] = jnp.zeros(acc_ref.shape, F32)

    s = _dot_nt(q_ref[...], k_ref[...])
    m_prev = m_ref[...]
    m_new = jnp.maximum(m_prev, jnp.max(s, axis=-1, keepdims=True))
    alpha = jnp.exp(m_prev - m_new)
    p = jnp.exp(s - m_new)
    l_ref[...] = alpha * l_ref[...] + jnp.sum(p, axis=-1, keepdims=True)
    acc_ref[...] = alpha * acc_ref[...] + _dot(p.astype(BF16), v_ref[...])
    m_ref[...] = m_new

    @pl.when(j == pl.num_programs(3) - 1)
    def _():
        o_ref[...] = (acc_ref[...] / l_ref[...]).astype(o_ref.dtype)


def attention(q, k, v, bsz, seq):
    m = q.shape[0]
    bq = _pick(seq, 1024)
    bk = _pick(seq, 1024)
    nq, nk = seq // bq, seq // bk
    return pl.pallas_call(
        _attn_kernel,
        grid=(bsz, MLA_HEADS, nq, nk),
        in_specs=[pl.BlockSpec((bq, QK_PAD), lambda b, h, i, j: (b * nq + i, h)),
                  pl.BlockSpec((bk, QK_PAD), lambda b, h, i, j: (b * nk + j, h)),
                  pl.BlockSpec((bk, VDIM), lambda b, h, i, j: (b * nk + j, h))],
        out_specs=pl.BlockSpec((bq, VDIM), lambda b, h, i, j: (b * nq + i, h)),
        out_shape=jax.ShapeDtypeStruct((m, WIDTH), BF16),
        scratch_shapes=[pltpu.VMEM((bq, 1), F32), pltpu.VMEM((bq, 1), F32),
                        pltpu.VMEM((bq, VDIM), F32)],
        compiler_params=_cparams(("parallel", "parallel", "parallel", "arbitrary")),
        name="mla_attn",
    )(q, k, v)


def _shifted(u, prev_row, next_row):
    n = u.shape[0]
    row = lax.broadcasted_iota(jnp.int32, (n, 1), 0)
    up = jnp.where(row == 0, prev_row, pltpu.roll(u, 1, axis=0))
    un = jnp.where(row == n - 1, next_row, pltpu.roll(u, n - 1, axis=0))
    return up, un


def _halo_specs(bm, width, nb, bsz):
    r8 = bm // 8
    last8 = bsz * nb * r8 - 1
    cur = pl.BlockSpec((bm, width), lambda b, i: (b * nb + i, 0))
    prev = pl.BlockSpec((8, width), lambda b, i: (jnp.maximum((b * nb + i) * r8 - 1, 0), 0))
    nxt = pl.BlockSpec((8, width), lambda b, i: (jnp.minimum((b * nb + i + 1) * r8, last8), 0))
    return cur, prev, nxt


def _rwkv_prep_kernel(u_ref, up_ref, un_ref, mu_ref, w0_ref, a0_ref, wup_ref, aup_ref, gup_ref,
                      kk_ref, ka_ref, rk_ref, ones_ref,
                      r_o, v_o, a_o, lw0_o, lw1_o, k0_o, k1_o, b0_o, b1_o, bonus_o, g_o):
    i = pl.program_id(1)
    u = u_ref[...]
    prev_row = jnp.where(i == 0, 0.0, up_ref[7:8, :])
    next_row = jnp.where(i == pl.num_programs(1) - 1, 0.0, un_ref[0:1, :])
    up, un = _shifted(u, prev_row, next_row)
    us = u + mu_ref[0:1, :] * (up - u) + mu_ref[1:2, :] * (un - u)
    r = us[:, 0:WIDTH]
    k = us[:, WIDTH:2 * WIDTH]
    v = us[:, 2 * WIDTH:3 * WIDTH]
    o = 3 * WIDTH
    wd = us[:, o:o + 2 * DECAY_LORA]
    ad = us[:, o + 128:o + 128 + 2 * AAA_LORA]
    gd = us[:, o + 256:o + 256 + GD_PAD]
    g_o[...] = _dot(jax.nn.sigmoid(gd).astype(BF16), gup_ref[...])
    z = -(w0_ref[...] + _dot(jnp.tanh(wd).astype(BF16), wup_ref[...]))
    softplus = jnp.maximum(z, 0.0) + jnp.log(1.0 + jnp.exp(-jnp.abs(z)))
    lw = -jnp.exp(-softplus - 0.5)
    al = jax.nn.sigmoid(a0_ref[...] + _dot(ad.astype(BF16), aup_ref[...]))
    ones_bd = ones_ref[...]
    kkr = k * kk_ref[...]
    nrm = jnp.sqrt(_seg_sum(kkr * kkr, ones_bd))
    kk = kkr / jnp.maximum(nrm, 1e-12)
    ka = ka_ref[...]
    a_0 = al[:, :WIDTH]
    a_1 = al[:, WIDTH:]
    k_0 = k * (1.0 + (a_0 - 1.0) * ka)
    k_1 = k * (1.0 + (a_1 - 1.0) * ka)
    r_o[...] = r
    v_o[...] = v
    a_o[...] = -kk
    lw0_o[...] = lw[:, :WIDTH]
    lw1_o[...] = lw[:, WIDTH:]
    k0_o[...] = k_0
    k1_o[...] = k_1
    b0_o[...] = kk * a_0
    b1_o[...] = kk * a_1
    bonus_o[...] = _seg_sum(r * rk_ref[...] * (k_0 + k_1), ones_bd) * v


def rwkv_prep(u_b, p, bsz, seq):
    m = u_b.shape[0]
    bm = _pick(seq, 128)
    nb = seq // bm
    cur, prev, nxt = _halo_specs(bm, RWKV_PAD, nb, bsz)
    full = lambda b, i: (0, 0)
    consts = [p["mu"], p["w0"], p["a0"], p["wup"], p["aup"], p["gup"], p["k_k"], p["k_a"], p["r_k"],
              p["ones_bd"]]
    out_spec = pl.BlockSpec((bm, WIDTH), lambda b, i: (b * nb + i, 0))
    return pl.pallas_call(
        _rwkv_prep_kernel,
        grid=(bsz, nb),
        in_specs=[cur, prev, nxt] + [pl.BlockSpec(c.shape, full) for c in consts],
        out_specs=[out_spec] * 11,
        out_shape=[jax.ShapeDtypeStruct((m, WIDTH), F32)] * 11,
        compiler_params=_cparams(("parallel", "arbitrary")),
        name="rwkv_prep",
    )(u_b, u_b, u_b, *consts)


def _dot3(a, b):
    ah, al = _split2(a)
    bh, bl = _split2(b)
    return _dot(ah, bh) + _dot(ah, bl) + _dot(al, bh)


def _unit_lower_inverse(a):
    n = a.shape[0]
    eye = (lax.broadcasted_iota(jnp.int32, (n, n), 0) == lax.broadcasted_iota(jnp.int32, (n, n), 1))
    t = a + eye.astype(F32)
    p = a
    steps = n.bit_length() - 2
    for _ in range(steps):
        p = _dot3(p, p)
        t = t + _dot3(p, t)
    return t


def _scan_kernel(r_ref, lw_ref, k_ref, v_ref, a_ref, b_ref, y_ref, s_ref, *, reverse, nchunk):
    @pl.when(pl.program_id(1) == 0)
    def _():
        s_ref[...] = jnp.zeros(s_ref.shape, F32)

    ti = lax.broadcasted_iota(jnp.int32, (CHUNK, CHUNK), 0)
    si = lax.broadcasted_iota(jnp.int32, (CHUNK, CHUNK), 1)
    if reverse:
        incl, strict = si >= ti, si > ti
    else:
        incl, strict = si <= ti, si < ti
    tri = incl.astype(BF16)
    last = 0 if reverse else CHUNK - 1

    def chunk(ci, carry):
        c = (nchunk - 1 - ci) if reverse else ci
        rows = pl.ds(pl.multiple_of(c * CHUNK, CHUNK), CHUNK)
        lw = lw_ref[rows, :]
        l_hi, l_mid, l_lo = _split3(lw)
        cum = _dot(tri, l_hi) + _dot(tri, l_mid) + _dot(tri, l_lo)
        gam = jnp.exp(cum)
        ginv = jnp.exp(-cum)
        rt = r_ref[rows, :] * gam
        at = a_ref[rows, :] * jnp.exp(cum - lw)
        kt = k_ref[rows, :] * ginv
        bt = b_ref[rows, :] * ginv
        v = v_ref[rows, :]
        g_end = gam[last:last + 1, :]
        for h in range(RW_HEADS):
            sl = slice(h * RW_N, (h + 1) * RW_N)
            at_h = at[:, sl].astype(BF16)
            rt_h = rt[:, sl].astype(BF16)
            bt_h = bt[:, sl].astype(BF16)
            kt_h = kt[:, sl].astype(BF16)
            v_h = v[:, sl].astype(BF16)
            ar = jnp.concatenate([at_h, rt_h], axis=0)
            gb = _dot_nt(ar, bt_h)
            gk = _dot_nt(ar, kt_h)
            a_ab = jnp.where(strict, gb[:CHUNK], 0.0)
            a_rb = jnp.where(incl, gb[CHUNK:], 0.0)
            a_ak = jnp.where(strict, gk[:CHUNK], 0.0)
            a_rk = jnp.where(incl, gk[CHUNK:], 0.0)
            t_inv = _unit_lower_inverse(a_ab)
            s0 = s_ref[h]
            s0b = s0.astype(BF16)
            hs = _dot_nt(ar, s0b)
            w_in = hs[:CHUNK] + _dot(a_ak.astype(BF16), v_h)
            u = _dot3(t_inv, w_in)
            ub = u.astype(BF16)
            y = hs[CHUNK:] + _dot(a_rb.astype(BF16), ub) + _dot(a_rk.astype(BF16), v_h)
            s_new = (s0 + _dot_tn(ub, bt_h) + _dot_tn(v_h, kt_h)) * g_end[:, sl]
            s_ref[h] = s_new
            y_ref[rows, sl] = y
        return carry

    lax.fori_loop(0, nchunk, chunk, 0)


def rwkv_scan(r, lw, k, v, a, b, bsz, seq, reverse):
    m = r.shape[0]
    bt = _pick(seq, 256)
    nb = seq // bt
    if reverse:
        imap = lambda bi, i: (bi * nb + nb - 1 - i, 0)
    else:
        imap = lambda bi, i: (bi * nb + i, 0)
    spec = pl.BlockSpec((bt, WIDTH), imap)
    return pl.pallas_call(
        functools.partial(_scan_kernel, reverse=reverse, nchunk=bt // CHUNK),
        grid=(bsz, nb),
        in_specs=[spec] * 6,
        out_specs=spec,
        out_shape=jax.ShapeDtypeStruct((m, WIDTH), F32),
        scratch_shapes=[pltpu.VMEM((RW_HEADS, RW_N, RW_N), F32)],
        compiler_params=_cparams(("parallel", "arbitrary")),
        name="rwkv_scan_bwd" if reverse else "rwkv_scan_fwd",
    )(r, lw, k, v, a, b)


def _rwkv_post_kernel(y0_ref, y1_ref, bonus_ref, g_ref, lg_ref, lb_ref, ones_ref, o_ref):
    ones_bd = ones_ref[...]
    y = y0_ref[...] + y1_ref[...]
    mu = _seg_sum(y, ones_bd) * (1.0 / RW_N)
    yc = y - mu
    var = _seg_sum(yc * yc, ones_bd) * (1.0 / RW_N)
    yn = yc * lax.rsqrt(var + LNX_EPS) * lg_ref[...] + lb_ref[...]
    o_ref[...] = ((yn + bonus_ref[...]) * g_ref[...]).astype(o_ref.dtype)


def rwkv_post(y0, y1, bonus, g, lg, lb, ones_bd):
    m = y0.shape[0]
    bm = _pick(m, 512)
    spec = pl.BlockSpec((bm, WIDTH), lambda i: (i, 0))
    full = lambda i: (0, 0)
    return pl.pallas_call(
        _rwkv_post_kernel,
        grid=(m // bm,),
        in_specs=[spec] * 4 + [pl.BlockSpec((1, WIDTH), full), pl.BlockSpec((1, WIDTH), full),
                               pl.BlockSpec((WIDTH, WIDTH), full)],
        out_specs=spec,
        out_shape=jax.ShapeDtypeStruct((m, WIDTH), BF16),
        compiler_params=_cparams(("parallel",)),
        name="rwkv_post",
    )(y0, y1, bonus, g, lg, lb, ones_bd)


def _conv_kernel(u_ref, up_ref, un_ref, w_ref, o_ref):
    i = pl.program_id(1)
    u = u_ref[...]
    hp = u[:, WIDTH:2 * WIDTH] * u[:, 2 * WIDTH:]
    pr = up_ref[7:8, :]
    nr = un_ref[0:1, :]
    prev_row = jnp.where(i == 0, 0.0, pr[:, WIDTH:2 * WIDTH] * pr[:, 2 * WIDTH:])
    next_row = jnp.where(i == pl.num_programs(1) - 1, 0.0, nr[:, WIDTH:2 * WIDTH] * nr[:, 2 * WIDTH:])
    hm, hn = _shifted(hp, prev_row, next_row)
    conv = w_ref[0:1, :] * hm + w_ref[1:2, :] * hp + w_ref[2:3, :] * hn
    o_ref[...] = (u[:, :WIDTH] * conv).astype(o_ref.dtype)


def conv_branch(u_c, conv_w, bsz, seq):
    m = u_c.shape[0]
    bm = _pick(seq, 512)
    nb = seq // bm
    cur, prev, nxt = _halo_specs(bm, CONV_COLS, nb, bsz)
    return pl.pallas_call(
        _conv_kernel,
        grid=(bsz, nb),
        in_specs=[cur, prev, nxt, pl.BlockSpec((3, WIDTH), lambda b, i: (0, 0))],
        out_specs=pl.BlockSpec((bm, WIDTH), lambda b, i: (b * nb + i, 0)),
        out_shape=jax.ShapeDtypeStruct((m, WIDTH), BF16),
        compiler_params=_cparams(("parallel", "arbitrary")),
        name="conv_branch",
    )(u_c, u_c, u_c, conv_w)


def _merge_kernel(ya_ref, yb_ref, yc_ref, g0_ref, g1_ref, g2_ref, w_ref, o_ref):
    acc = jax.nn.sigmoid(g0_ref[...]) * _dot(ya_ref[...], w_ref[0])
    acc = acc + jax.nn.sigmoid(g1_ref[...]) * _dot(yb_ref[...], w_ref[1])
    acc = acc + jax.nn.sigmoid(g2_ref[...]) * _dot(yc_ref[...], w_ref[2])
    o_ref[...] = acc.astype(o_ref.dtype)


def merge_branches(ya, yb, yc, u_g, w_branch):
    m = ya.shape[0]
    bm = _pick(m, 512)
    bn = 1024
    nj = D_MODEL // bn
    yspec = pl.BlockSpec((bm, WIDTH), lambda i, j: (i, 0))
    gspecs = [pl.BlockSpec((bm, bn), functools.partial(lambda i, j, t: (i, t * nj + j), t=t))
              for t in range(3)]
    return pl.pallas_call(
        _merge_kernel,
        grid=(m // bm, nj),
        in_specs=[yspec] * 3 + gspecs + [pl.BlockSpec((3, WIDTH, bn), lambda i, j: (0, 0, j))],
        out_specs=pl.BlockSpec((bm, bn), lambda i, j: (i, j)),
        out_shape=jax.ShapeDtypeStruct((m, D_MODEL), BF16),
        compiler_params=_cparams(("parallel", "arbitrary")),
        name="merge",
    )(ya, yb, yc, u_g, u_g, u_g, w_branch)


def _mm_res_ln_kernel(x_ref, w_ref, res_ref, g_ref, b_ref, o_ref, ob_ref, acc_ref):
    kk = pl.program_id(1)

    @pl.when(kk == 0)
    def _():
        acc_ref[...] = jnp.zeros(acc_ref.shape, F32)

    acc_ref[...] += _dot(x_ref[...], w_ref[...])

    @pl.when(kk == pl.num_programs(1) - 1)
    def _():
        y = _ln_math(ALPHA * res_ref[...] + acc_ref[...], g_ref[...], b_ref[...])
        o_ref[...] = y
        ob_ref[...] = y.astype(BF16)


def matmul_res_ln(x, w, res, g, b, bk, name):
    m, k = x.shape
    n = w.shape[1]
    bm = _pick(m, 512)
    return pl.pallas_call(
        _mm_res_ln_kernel,
        grid=(m // bm, k // bk),
        in_specs=[pl.BlockSpec((bm, bk), lambda i, j: (i, j)),
                  pl.BlockSpec((bk, n), lambda i, j: (j, 0)),
                  pl.BlockSpec((bm, n), lambda i, j: (i, 0)),
                  pl.BlockSpec((1, n), lambda i, j: (0, 0)),
                  pl.BlockSpec((1, n), lambda i, j: (0, 0))],
        out_specs=[pl.BlockSpec((bm, n), lambda i, j: (i, 0)),
                   pl.BlockSpec((bm, n), lambda i, j: (i, 0))],
        out_shape=[jax.ShapeDtypeStruct((m, n), F32), jax.ShapeDtypeStruct((m, n), BF16)],
        scratch_shapes=[pltpu.VMEM((bm, n), F32)],
        compiler_params=_cparams(("parallel", "arbitrary")),
        name=name,
    )(x, w, res, g, b)


def _ffn_in_kernel(x_ref, wg_ref, wu_ref, o_ref):
    x = x_ref[...]
    gate = _dot(x, wg_ref[...])
    up = _dot(x, wu_ref[...])
    o_ref[...] = (gate * jax.nn.sigmoid(gate) * up).astype(o_ref.dtype)


def ffn_in(x, w):
    m, k = x.shape
    bm = _pick(m, 1024)
    bn = 512
    nj = D_FF // bn
    return pl.pallas_call(
        _ffn_in_kernel,
        grid=(m // bm, nj),
        in_specs=[pl.BlockSpec((bm, k), lambda i, j: (i, 0)),
                  pl.BlockSpec((k, bn), lambda i, j: (0, j)),
                  pl.BlockSpec((k, bn), lambda i, j: (0, nj + j))],
        out_specs=pl.BlockSpec((bm, bn), lambda i, j: (i, j)),
        out_shape=jax.ShapeDtypeStruct((m, D_FF), BF16),
        compiler_params=_cparams(("parallel", "arbitrary")),
        name="ffn_in",
    )(x, w, w)


def _prep_params(w_in, q_norm_g, w_uq, kv_norm_g, w_ukv, rwkv_mu, rwkv_w0, rwkv_w_up, rwkv_a0,
                 rwkv_a_up, rwkv_g_up, rwkv_k_k, rwkv_k_a, rwkv_r_k, rwkv_lnx_g, rwkv_lnx_b,
                 conv_w, w_branch, w_out, ln1_g, ln1_b, w_ffn_in, w_ffn_out, ln2_g, ln2_b):
    nl = w_in.shape[0]
    z = lambda *s: jnp.zeros(s, F32)
    o = MLA_COLS
    kr = w_in[:, :, o - ROPE:o]
    kr_rot = jnp.concatenate([-kr[..., ROPE // 2:], kr[..., :ROPE // 2]], axis=-1)
    w_mla = jnp.concatenate([w_in[:, :, :o], z(nl, D_MODEL, 64), kr_rot, z(nl, D_MODEL, 64)], axis=-1)
    w_rw = jnp.concatenate([w_in[:, :, o:o + RWKV_COLS], z(nl, D_MODEL, RWKV_PAD - RWKV_COLS)], axis=-1)
    o += RWKV_COLS
    w_cv = w_in[:, :, o:o + CONV_COLS]
    o += CONV_COLS
    w_gt = w_in[:, :, o:]

    wq = w_uq.reshape(nl, Q_LORA, MLA_HEADS, NOPE + ROPE)
    x1 = wq[..., NOPE:NOPE + ROPE // 2]
    x2 = wq[..., NOPE + ROPE // 2:]
    pad = z(nl, Q_LORA, MLA_HEADS, 128 - ROPE)
    wq_all = jnp.concatenate([wq[..., :NOPE].reshape(nl, Q_LORA, WIDTH),
                              jnp.concatenate([x1, x2, pad], -1).reshape(nl, Q_LORA, WIDTH),
                              jnp.concatenate([-x2, x1, pad], -1).reshape(nl, Q_LORA, WIDTH)], axis=-1)
    wkv = w_ukv.reshape(nl, KV_LORA, MLA_HEADS, NOPE + VDIM)
    wkv_all = jnp.concatenate([wkv[..., :NOPE].reshape(nl, KV_LORA, WIDTH),
                               wkv[..., NOPE:].reshape(nl, KV_LORA, WIDTH)], axis=-1)

    zl = z(nl, DECAY_LORA, WIDTH)
    wup = jnp.concatenate([jnp.concatenate([rwkv_w_up[:, 0], zl], -1),
                           jnp.concatenate([zl, rwkv_w_up[:, 1]], -1)], axis=1)
    aup = jnp.concatenate([jnp.concatenate([rwkv_a_up[:, 0], zl], -1),
                           jnp.concatenate([zl, rwkv_a_up[:, 1]], -1)], axis=1)
    gup = jnp.concatenate([rwkv_g_up, z(nl, GD_PAD - GATE_LORA, WIDTH)], axis=1)
    mu = jnp.concatenate([rwkv_mu, z(nl, 2, RWKV_PAD - RWKV_COLS)], axis=-1)
    head = jnp.arange(WIDTH) // RW_N
    ones_bd = (head[:, None] == head[None, :]).astype(BF16)
    row = lambda t: t.reshape(nl, 1, -1)
    return dict(
        w_mla=w_mla.astype(BF16), w_rw=w_rw.astype(BF16), w_cv=w_cv.astype(BF16), w_gt=w_gt.astype(BF16),
        qg=row(q_norm_g), kvg=row(kv_norm_g), wq=wq_all.astype(BF16), wkv=wkv_all.astype(BF16),
        mu=mu, w0=row(rwkv_w0), a0=row(rwkv_a0), wup=wup.astype(BF16), aup=aup.astype(BF16),
        gup=gup.astype(BF16), k_k=row(rwkv_k_k), k_a=row(rwkv_k_a), r_k=row(rwkv_r_k),
        lnx_g=row(rwkv_lnx_g), lnx_b=row(rwkv_lnx_b), conv_w=conv_w,
        w_branch=w_branch.astype(BF16), w_out=w_out.astype(BF16),
        ln1_g=row(ln1_g), ln1_b=row(ln1_b), w_ffn_in=w_ffn_in.astype(BF16),
        w_ffn_out=w_ffn_out.astype(BF16), ln2_g=row(ln2_g), ln2_b=row(ln2_b),
    ), ones_bd


def _rope_tables(seq):
    pos = jnp.arange(seq, dtype=F32)
    inv = ROPE_THETA ** (-jnp.arange(0, ROPE, 2, dtype=F32) / ROPE)
    ang = pos[:, None] * inv[None, :]
    c, s = jnp.cos(ang), jnp.sin(ang)
    zpad = jnp.zeros((seq, 128 - ROPE), F32)
    return jnp.concatenate([c, c, zpad], -1), jnp.concatenate([s, s, zpad], -1)


def _layer(x, xb, p, ones_bd, cosz, sinz, bsz, seq):
    u_a = matmul(xb, p["w_mla"], MLA_PAD, name="in_mla")
    u_b = matmul(xb, p["w_rw"], RWKV_PAD // 2, name="in_rwkv")
    u_c = matmul(xb, p["w_cv"], 1024, name="in_conv")
    u_g = matmul(xb, p["w_gt"], 1024, name="in_gate")

    q, k, v = mla_prep(u_a, p["qg"], p["kvg"], p["wq"], p["wkv"], cosz, sinz, seq)
    y_a = attention(q, k, v, bsz, seq)

    pp = dict(p, ones_bd=ones_bd)
    r, vv, a, lw0, lw1, k0, k1, b0, b1, bonus, g = rwkv_prep(u_b, pp, bsz, seq)
    y0 = rwkv_scan(r, lw0, k0, vv, a, b0, bsz, seq, reverse=False)
    y1 = rwkv_scan(r, lw1, k1, vv, a, b1, bsz, seq, reverse=True)
    y_b = rwkv_post(y0, y1, bonus, g, p["lnx_g"], p["lnx_b"], ones_bd)

    y_c = conv_branch(u_c, p["conv_w"], bsz, seq)

    merged = merge_branches(y_a, y_b, y_c, u_g, p["w_branch"])
    h, hb = matmul_res_ln(merged, p["w_out"], x, p["ln1_g"], p["ln1_b"], 512, "out_proj_ln")
    f = ffn_in(hb, p["w_ffn_in"])
    return matmul_res_ln(f, p["w_ffn_out"], h, p["ln2_g"], p["ln2_b"], 512, "ffn_out_ln")


def _trunk(x3, ln_g, ln_b, params, ones_bd):
    bsz, seq, d = x3.shape
    cosz, sinz = _rope_tables(seq)
    x, xb = layer_norm_in(x3.reshape(bsz * seq, d), ln_g.reshape(1, d), ln_b.reshape(1, d))
    for l in range(DEPTH):
        p = {name: t[l] for name, t in params.items()}
        x, xb = _layer(x, xb, p, ones_bd, cosz, sinz, bsz, seq)
    return x.reshape(bsz, seq, d)


def kernel(x_prompt, x_sample, ln_in_g, ln_in_b, w_in, q_norm_g, w_uq, kv_norm_g, w_ukv, rwkv_mu, rwkv_w0, rwkv_w_up, rwkv_a0, rwkv_a_up, rwkv_g_up, rwkv_k_k, rwkv_k_a, rwkv_r_k, rwkv_lnx_g, rwkv_lnx_b, conv_w, w_branch, w_out, ln1_g, ln1_b, w_ffn_in, w_ffn_out, ln2_g, ln2_b):
    params, ones_bd = _prep_params(w_in, q_norm_g, w_uq, kv_norm_g, w_ukv, rwkv_mu, rwkv_w0, rwkv_w_up,
                                   rwkv_a0, rwkv_a_up, rwkv_g_up, rwkv_k_k, rwkv_k_a, rwkv_r_k,
                                   rwkv_lnx_g, rwkv_lnx_b, conv_w, w_branch, w_out, ln1_g, ln1_b,
                                   w_ffn_in, w_ffn_out, ln2_g, ln2_b)
    y_prompt = _trunk(x_prompt, ln_in_g, ln_in_b, params, ones_bd)
    y_sample = _trunk(x_sample, ln_in_g, ln_in_b, params, ones_bd)
    return (y_prompt, y_sample)
```

```python
import functools

import jax
import jax.numpy as jnp
from jax import lax
from jax.experimental import pallas as pl
from jax.experimental.pallas import tpu as pltpu

F32 = jnp.float32
BF16 = jnp.bfloat16

D_MODEL = 2048
DEPTH = 4
WIDTH = D_MODEL // 2
MLA_HEADS = 8
NOPE = 128
ROPE = 64
VDIM = 128
Q_LORA = 768
KV_LORA = 512
ROPE_THETA = 10000.0
RW_HEADS = 16
RW_N = 64
DECAY_LORA = 64
AAA_LORA = 64
GATE_LORA = 160
D_FF = 5632
LN_EPS = 1e-5
RMS_EPS = 1e-6
LNX_EPS = 1e-5 * RW_N
ALPHA = (2 * DEPTH) ** 0.25
MLA_COLS = Q_LORA + KV_LORA + ROPE
RWKV_COLS = 3 * WIDTH + 2 * DECAY_LORA + 2 * AAA_LORA + GATE_LORA
CONV_COLS = 3 * WIDTH
GATE_COLS = 3 * D_MODEL
MLA_PAD = 1536
RWKV_PAD = 3584
GD_PAD = 256
QK_PAD = 256
CHUNK = 64

VMEM_LIMIT = 56 * 1024 * 1024


def _cparams(sem):
    return pltpu.CompilerParams(dimension_semantics=sem, vmem_limit_bytes=VMEM_LIMIT)


def _dot(a, b):
    return jnp.dot(a, b, preferred_element_type=F32)


def _dot_nt(a, b):
    return lax.dot_general(a, b, (((1,), (1,)), ((), ())), preferred_element_type=F32)


def _dot_tn(a, b):
    return lax.dot_general(a, b, (((0,), (0,)), ((), ())), preferred_element_type=F32)


def _split2(x):
    hi = x.astype(BF16)
    lo = (x - hi.astype(F32)).astype(BF16)
    return hi, lo


def _split3(x):
    hi = x.astype(BF16)
    r1 = x - hi.astype(F32)
    mid = r1.astype(BF16)
    lo = (r1 - mid.astype(F32)).astype(BF16)
    return hi, mid, lo


def _seg_sum(x, ones_bd):
    hi, lo = _split2(x)
    return _dot(hi, ones_bd) + _dot(lo, ones_bd)


def _pick(n, target):
    b = min(n, target)
    while n % b:
        b //= 2
    return b


def _ln_math(x, g, b):
    mu = jnp.mean(x, axis=-1, keepdims=True)
    xc = x - mu
    var = jnp.mean(xc * xc, axis=-1, keepdims=True)
    return xc * lax.rsqrt(var + LN_EPS) * g + b


def _ln_kernel(x_ref, g_ref, b_ref, o_ref, ob_ref):
    y = _ln_math(x_ref[...], g_ref[...], b_ref[...])
    o_ref[...] = y
    ob_ref[...] = y.astype(BF16)


def layer_norm_in(x, g, b):
    m, d = x.shape
    bm = _pick(m, 512)
    return pl.pallas_call(
        _ln_kernel,
        grid=(m // bm,),
        in_specs=[pl.BlockSpec((bm, d), lambda i: (i, 0)),
                  pl.BlockSpec((1, d), lambda i: (0, 0)),
                  pl.BlockSpec((1, d), lambda i: (0, 0))],
        out_specs=[pl.BlockSpec((bm, d), lambda i: (i, 0)),
                   pl.BlockSpec((bm, d), lambda i: (i, 0))],
        out_shape=[jax.ShapeDtypeStruct((m, d), F32), jax.ShapeDtypeStruct((m, d), BF16)],
        compiler_params=_cparams(("parallel",)),
        name="ln_in",
    )(x, g, b)


def _mm_kernel(x_ref, w_ref, o_ref):
    o_ref[...] = _dot(x_ref[...], w_ref[...]).astype(o_ref.dtype)


def matmul(x, w, bn, out_dtype=F32, name="mm"):
    m, k = x.shape
    n = w.shape[1]
    bm = _pick(m, 1024)
    return pl.pallas_call(
        _mm_kernel,
        grid=(m // bm, n // bn),
        in_specs=[pl.BlockSpec((bm, k), lambda i, j: (i, 0)),
                  pl.BlockSpec((k, bn), lambda i, j: (0, j))],
        out_specs=pl.BlockSpec((bm, bn), lambda i, j: (i, j)),
        out_shape=jax.ShapeDtypeStruct((m, n), out_dtype),
        compiler_params=_cparams(("parallel", "arbitrary")),
        name=name,
    )(x, w)


def _rms(x, g):
    return x * lax.rsqrt(jnp.mean(x * x, axis=-1, keepdims=True) + RMS_EPS) * g


def _mla_prep_kernel(u_ref, qg_ref, kvg_ref, wq_ref, wkv_ref, cos_ref, sin_ref,
                     q_ref, k_ref, v_ref):
    scale = (NOPE + ROPE) ** -0.5
    cz = cos_ref[...]
    sz = sin_ref[...]
    cq = _rms(u_ref[:, :Q_LORA], qg_ref[...]).astype(BF16)
    pq = _dot(cq, wq_ref[...])
    ckv = _rms(u_ref[:, Q_LORA:Q_LORA + KV_LORA], kvg_ref[...]).astype(BF16)
    pkv = _dot(ckv, wkv_ref[...])
    o = Q_LORA + KV_LORA
    krz = (u_ref[:, o:o + 128] * cz + u_ref[:, o + 128:o + 256] * sz).astype(BF16)
    lane = lax.broadcasted_iota(jnp.int32, (cz.shape[0], 128), 1)
    one_col = jnp.where(lane == 0, 1.0, 0.0).astype(BF16)
    for h in range(MLA_HEADS):
        a, b = h * 128, (h + 1) * 128
        q_ref[:, h * QK_PAD:h * QK_PAD + 128] = (pq[:, a:b] * scale).astype(BF16)
        q_ref[:, h * QK_PAD + 128:(h + 1) * QK_PAD] = (
            (pq[:, WIDTH + a:WIDTH + b] * cz + pq[:, 2 * WIDTH + a:2 * WIDTH + b] * sz) * scale
        ).astype(BF16)
        k_ref[:, h * QK_PAD:h * QK_PAD + 128] = pkv[:, a:b].astype(BF16)
        k_ref[:, h * QK_PAD + 128:(h + 1) * QK_PAD] = krz
        v_ref[:, h * QK_PAD:h * QK_PAD + 128] = pkv[:, WIDTH + a:WIDTH + b].astype(BF16)
        v_ref[:, h * QK_PAD + 128:(h + 1) * QK_PAD] = one_col


def mla_prep(u_a, qg, kvg, wq, wkv, cosz, sinz, seq):
    m = u_a.shape[0]
    bm = _pick(seq, 512)
    nb = seq // bm
    full = lambda i: (0, 0)
    return pl.pallas_call(
        _mla_prep_kernel,
        grid=(m // bm,),
        in_specs=[pl.BlockSpec((bm, MLA_PAD), lambda i: (i, 0)),
                  pl.BlockSpec((1, Q_LORA), full),
                  pl.BlockSpec((1, KV_LORA), full),
                  pl.BlockSpec(wq.shape, full),
                  pl.BlockSpec(wkv.shape, full),
                  pl.BlockSpec((bm, 128), lambda i: (i % nb, 0)),
                  pl.BlockSpec((bm, 128), lambda i: (i % nb, 0))],
        out_specs=[pl.BlockSpec((bm, MLA_HEADS * QK_PAD), lambda i: (i, 0)),
                   pl.BlockSpec((bm, MLA_HEADS * QK_PAD), lambda i: (i, 0)),
                   pl.BlockSpec((bm, MLA_HEADS * QK_PAD), lambda i: (i, 0))],
        out_shape=[jax.ShapeDtypeStruct((m, MLA_HEADS * QK_PAD), BF16)] * 3,
        compiler_params=_cparams(("parallel",)),
        name="mla_prep",
    )(u_a, qg, kvg, wq, wkv, cosz, sinz)


def _attn_kernel(q_ref, k_ref, v_ref, o_ref, *, nkv, bk):
    q = q_ref[...]
    m = jnp.full((q.shape[0], 1), -jnp.inf, F32)
    acc = jnp.zeros((q.shape[0], QK_PAD), F32)
    s = _dot_nt(q, k_ref[0:bk, :])
    for j in range(nkv):
        if j + 1 < nkv:
            s_next = _dot_nt(q, k_ref[(j + 1) * bk:(j + 2) * bk, :])
        m_new = jnp.maximum(m, jnp.max(s, axis=-1, keepdims=True))
        alpha = jnp.exp(m - m_new)
        p = jnp.exp((s - m_new).astype(BF16))
        acc = alpha * acc + _dot(p, v_ref[j * bk:(j + 1) * bk, :])
        m = m_new
        if j + 1 < nkv:
            s = s_next
    o_ref[...] = (acc[:, :VDIM] / acc[:, VDIM:VDIM + 1]).astype(o_ref.dtype)


def attention(q, k, v, bsz, seq):
    m = q.shape[0]
    bq = _pick(seq, 1024)
    bk = _pick(seq, 1024)
    nq = seq // bq
    return pl.pallas_call(
        functools.partial(_attn_kernel, nkv=seq // bk, bk=bk),
        grid=(bsz, MLA_HEADS, nq),
        in_specs=[pl.BlockSpec((bq, QK_PAD), lambda b, h, i: (b * nq + i, h)),
                  pl.BlockSpec((seq, QK_PAD), lambda b, h, i: (b, h)),
                  pl.BlockSpec((seq, QK_PAD), lambda b, h, i: (b, h))],
        out_specs=pl.BlockSpec((bq, VDIM), lambda b, h, i: (b * nq + i, h)),
        out_shape=jax.ShapeDtypeStruct((m, WIDTH), BF16),
        compiler_params=_cparams(("parallel", "parallel", "arbitrary")),
        name="mla_attn",
    )(q, k, v)


def _shifted(u, prev_row, next_row):
    n = u.shape[0]
    row = lax.broadcasted_iota(jnp.int32, (n, 1), 0)
    up = jnp.where(row == 0, prev_row, pltpu.roll(u, 1, axis=0))
    un = jnp.where(row == n - 1, next_row, pltpu.roll(u, n - 1, axis=0))
    return up, un


def _halo_specs(bm, width, nb, bsz):
    r8 = bm // 8
    last8 = bsz * nb * r8 - 1
    cur = pl.BlockSpec((bm, width), lambda b, i: (b * nb + i, 0))
    prev = pl.BlockSpec((8, width), lambda b, i: (jnp.maximum((b * nb + i) * r8 - 1, 0), 0))
    nxt = pl.BlockSpec((8, width), lambda b, i: (jnp.minimum((b * nb + i + 1) * r8, last8), 0))
    return cur, prev, nxt


def _rwkv_prep_kernel(u_ref, up_ref, un_ref, mu_ref, w0_ref, a0_ref, wup_ref, aup_ref, gup_ref,
                      kk_ref, ka_ref, rk_ref, ones_ref,
                      r_o, v_o, a_o, lw0_o, lw1_o, k0_o, k1_o, b0_o, b1_o, bonus_o, g_o):
    i = pl.program_id(1)
    u = u_ref[...]
    prev_row = jnp.where(i == 0, 0.0, up_ref[7:8, :])
    next_row = jnp.where(i == pl.num_programs(1) - 1, 0.0, un_ref[0:1, :])
    up, un = _shifted(u, prev_row, next_row)
    us = u + mu_ref[0:1, :] * (up - u) + mu_ref[1:2, :] * (un - u)
    r = us[:, 0:WIDTH]
    k = us[:, WIDTH:2 * WIDTH]
    v = us[:, 2 * WIDTH:3 * WIDTH]
    o = 3 * WIDTH
    wd = us[:, o:o + 2 * DECAY_LORA]
    ad = us[:, o + 128:o + 128 + 2 * AAA_LORA]
    gd = us[:, o + 256:o + 256 + GD_PAD]
    g_o[...] = _dot(jax.nn.sigmoid(gd).astype(BF16), gup_ref[...])
    z = -(w0_ref[...] + _dot(jnp.tanh(wd).astype(BF16), wup_ref[...]))
    softplus = jnp.maximum(z, 0.0) + jnp.log(1.0 + jnp.exp(-jnp.abs(z)))
    lw = -jnp.exp(-softplus - 0.5)
    al = jax.nn.sigmoid(a0_ref[...] + _dot(ad.astype(BF16), aup_ref[...]))
    ones_bd = ones_ref[...]
    kkr = k * kk_ref[...]
    nrm = jnp.sqrt(_seg_sum(kkr * kkr, ones_bd))
    kk = kkr / jnp.maximum(nrm, 1e-12)
    ka = ka_ref[...]
    a_0 = al[:, :WIDTH]
    a_1 = al[:, WIDTH:]
    k_0 = k * (1.0 + (a_0 - 1.0) * ka)
    k_1 = k * (1.0 + (a_1 - 1.0) * ka)
    r_o[...] = r
    v_o[...] = v
    a_o[...] = -kk
    lw0_o[...] = lw[:, :WIDTH]
    lw1_o[...] = lw[:, WIDTH:]
    k0_o[...] = k_0
    k1_o[...] = k_1
    b0_o[...] = kk * a_0
    b1_o[...] = kk * a_1
    bonus_o[...] = _seg_sum(r * rk_ref[...] * (k_0 + k_1), ones_bd) * v


def rwkv_prep(u_b, p, bsz, seq):
    m = u_b.shape[0]
    bm = _pick(seq, 128)
    nb = seq // bm
    cur, prev, nxt = _halo_specs(bm, RWKV_PAD, nb, bsz)
    full = lambda b, i: (0, 0)
    consts = [p["mu"], p["w0"], p["a0"], p["wup"], p["aup"], p["gup"], p["k_k"], p["k_a"], p["r_k"],
              p["ones_bd"]]
    out_spec = pl.BlockSpec((bm, WIDTH), lambda b, i: (b * nb + i, 0))
    return pl.pallas_call(
        _rwkv_prep_kernel,
        grid=(bsz, nb),
        in_specs=[cur, prev, nxt] + [pl.BlockSpec(c.shape, full) for c in consts],
        out_specs=[out_spec] * 11,
        out_shape=[jax.ShapeDtypeStruct((m, WIDTH), F32)] * 11,
        compiler_params=_cparams(("parallel", "arbitrary")),
        name="rwkv_prep",
    )(u_b, u_b, u_b, *consts)


def _unit_lower_inverse(a_heads):
    n = a_heads[0].shape[0]
    eye = (lax.broadcasted_iota(jnp.int32, (n, n), 0)
           == lax.broadcasted_iota(jnp.int32, (n, n), 1)).astype(F32)
    t = [a + eye for a in a_heads]
    pb = [a.astype(BF16) for a in a_heads]
    for _ in range(n.bit_length() - 2):
        pb = [_dot(x, x).astype(BF16) for x in pb]
        t = [ti + _dot(x, ti.astype(BF16)) for x, ti in zip(pb, t)]
    return t


def _scan_kernel(r_ref, lw_ref, k_ref, v_ref, a_ref, b_ref, y_ref, s_ref, *, reverse, nchunk):
    @pl.when(pl.program_id(1) == 0)
    def _():
        s_ref[...] = jnp.zeros(s_ref.shape, F32)

    ti = lax.broadcasted_iota(jnp.int32, (CHUNK, CHUNK), 0)
    si = lax.broadcasted_iota(jnp.int32, (CHUNK, CHUNK), 1)
    if reverse:
        incl, strict = si >= ti, si > ti
    else:
        incl, strict = si <= ti, si < ti
    tri = incl.astype(BF16)
    last = 0 if reverse else CHUNK - 1

    def chunk(ci, carry):
        c = (nchunk - 1 - ci) if reverse else ci
        rows = pl.ds(pl.multiple_of(c * CHUNK, CHUNK), CHUNK)
        lw = lw_ref[rows, :]
        l_hi, l_mid, l_lo = _split3(lw)
        cum = _dot(tri, l_hi) + _dot(tri, l_mid) + _dot(tri, l_lo)
        gam = jnp.exp(cum)
        ginv = jnp.exp(-cum)
        rt = r_ref[rows, :] * gam
        at = a_ref[rows, :] * jnp.exp(cum - lw)
        kt = k_ref[rows, :] * ginv
        bt = b_ref[rows, :] * ginv
        v = v_ref[rows, :]
        g_end = gam[last:last + 1, :]
        heads = range(RW_HEADS)
        sls = [slice(h * RW_N, (h + 1) * RW_N) for h in heads]
        bt_h = [bt[:, sl].astype(BF16) for sl in sls]
        kt_h = [kt[:, sl].astype(BF16) for sl in sls]
        v_h = [v[:, sl].astype(BF16) for sl in sls]
        ar = [jnp.concatenate([at[:, sl].astype(BF16), rt[:, sl].astype(BF16)], axis=0) for sl in sls]
        s0 = [s_ref[h] for h in heads]
        gb = [_dot_nt(ar[h], bt_h[h]) for h in heads]
        gk = [_dot_nt(ar[h], kt_h[h]) for h in heads]
        hs = [_dot_nt(ar[h], s0[h].astype(BF16)) for h in heads]
        a_ak = [jnp.where(strict, gk[h][:CHUNK], 0.0).astype(BF16) for h in heads]
        w_in = [hs[h][:CHUNK] + _dot(a_ak[h], v_h[h]) for h in heads]
        t_inv = _unit_lower_inverse([jnp.where(strict, gb[h][:CHUNK], 0.0) for h in heads])
        ub = [_dot(t_inv[h].astype(BF16), w_in[h].astype(BF16)).astype(BF16) for h in heads]
        a_rb = [jnp.where(incl, gb[h][CHUNK:], 0.0).astype(BF16) for h in heads]
        a_rk = [jnp.where(incl, gk[h][CHUNK:], 0.0).astype(BF16) for h in heads]
        y = [hs[h][CHUNK:] + _dot(a_rb[h], ub[h]) + _dot(a_rk[h], v_h[h]) for h in heads]
        ds = [_dot_tn(ub[h], bt_h[h]) + _dot_tn(v_h[h], kt_h[h]) for h in heads]
        for h in heads:
            s_ref[h] = (s0[h] + ds[h]) * g_end[:, sls[h]]
            y_ref[rows, sls[h]] = y[h]
        return carry

    lax.fori_loop(0, nchunk, chunk, 0)


def rwkv_scan(r, lw, k, v, a, b, bsz, seq, reverse):
    m = r.shape[0]
    bt = _pick(seq, 256)
    nb = seq // bt
    if reverse:
        imap = lambda bi, i: (bi * nb + nb - 1 - i, 0)
    else:
        imap = lambda bi, i: (bi * nb + i, 0)
    spec = pl.BlockSpec((bt, WIDTH), imap)
    return pl.pallas_call(
        functools.partial(_scan_kernel, reverse=reverse, nchunk=bt // CHUNK),
        grid=(bsz, nb),
        in_specs=[spec] * 6,
        out_specs=spec,
        out_shape=jax.ShapeDtypeStruct((m, WIDTH), F32),
        scratch_shapes=[pltpu.VMEM((RW_HEADS, RW_N, RW_N), F32)],
        compiler_params=_cparams(("parallel", "arbitrary")),
        name="rwkv_scan_bwd" if reverse else "rwkv_scan_fwd",
    )(r, lw, k, v, a, b)


def _rwkv_post_kernel(y0_ref, y1_ref, bonus_ref, g_ref, lg_ref, lb_ref, ones_ref, o_ref):
    ones_bd = ones_ref[...]
    y = y0_ref[...] + y1_ref[...]
    mu = _seg_sum(y, ones_bd) * (1.0 / RW_N)
    yc = y - mu
    var = _seg_sum(yc * yc, ones_bd) * (1.0 / RW_N)
    yn = yc * lax.rsqrt(var + LNX_EPS) * lg_ref[...] + lb_ref[...]
    o_ref[...] = ((yn + bonus_ref[...]) * g_ref[...]).astype(o_ref.dtype)


def rwkv_post(y0, y1, bonus, g, lg, lb, ones_bd):
    m = y0.shape[0]
    bm = _pick(m, 512)
    spec = pl.BlockSpec((bm, WIDTH), lambda i: (i, 0))
    full = lambda i: (0, 0)
    return pl.pallas_call(
        _rwkv_post_kernel,
        grid=(m // bm,),
        in_specs=[spec] * 4 + [pl.BlockSpec((1, WIDTH), full), pl.BlockSpec((1, WIDTH), full),
                               pl.BlockSpec((WIDTH, WIDTH), full)],
        out_specs=spec,
        out_shape=jax.ShapeDtypeStruct((m, WIDTH), BF16),
        compiler_params=_cparams(("parallel",)),
        name="rwkv_post",
    )(y0, y1, bonus, g, lg, lb, ones_bd)


def _conv_kernel(u_ref, up_ref, un_ref, w_ref, o_ref):
    i = pl.program_id(1)
    u = u_ref[...]
    hp = u[:, WIDTH:2 * WIDTH] * u[:, 2 * WIDTH:]
    pr = up_ref[7:8, :]
    nr = un_ref[0:1, :]
    prev_row = jnp.where(i == 0, 0.0, pr[:, WIDTH:2 * WIDTH] * pr[:, 2 * WIDTH:])
    next_row = jnp.where(i == pl.num_programs(1) - 1, 0.0, nr[:, WIDTH:2 * WIDTH] * nr[:, 2 * WIDTH:])
    hm, hn = _shifted(hp, prev_row, next_row)
    conv = w_ref[0:1, :] * hm + w_ref[1:2, :] * hp + w_ref[2:3, :] * hn
    o_ref[...] = (u[:, :WIDTH] * conv).astype(o_ref.dtype)


def conv_branch(u_c, conv_w, bsz, seq):
    m = u_c.shape[0]
    bm = _pick(seq, 512)
    nb = seq // bm
    cur, prev, nxt = _halo_specs(bm, CONV_COLS, nb, bsz)
    return pl.pallas_call(
        _conv_kernel,
        grid=(bsz, nb),
        in_specs=[cur, prev, nxt, pl.BlockSpec((3, WIDTH), lambda b, i: (0, 0))],
        out_specs=pl.BlockSpec((bm, WIDTH), lambda b, i: (b * nb + i, 0)),
        out_shape=jax.ShapeDtypeStruct((m, WIDTH), BF16),
        compiler_params=_cparams(("parallel", "arbitrary")),
        name="conv_branch",
    )(u_c, u_c, u_c, conv_w)


def _merge_kernel(ya_ref, yb_ref, yc_ref, g0_ref, g1_ref, g2_ref, w_ref, o_ref):
    acc = jax.nn.sigmoid(g0_ref[...]) * _dot(ya_ref[...], w_ref[0])
    acc = acc + jax.nn.sigmoid(g1_ref[...]) * _dot(yb_ref[...], w_ref[1])
    acc = acc + jax.nn.sigmoid(g2_ref[...]) * _dot(yc_ref[...], w_ref[2])
    o_ref[...] = acc.astype(o_ref.dtype)


def merge_branches(ya, yb, yc, u_g, w_branch):
    m = ya.shape[0]
    bm = _pick(m, 512)
    bn = 1024
    nj = D_MODEL // bn
    yspec = pl.BlockSpec((bm, WIDTH), lambda i, j: (i, 0))
    gspecs = [pl.BlockSpec((bm, bn), functools.partial(lambda i, j, t: (i, t * nj + j), t=t))
              for t in range(3)]
    return pl.pallas_call(
        _merge_kernel,
        grid=(m // bm, nj),
        in_specs=[yspec] * 3 + gspecs + [pl.BlockSpec((3, WIDTH, bn), lambda i, j: (0, 0, j))],
        out_specs=pl.BlockSpec((bm, bn), lambda i, j: (i, j)),
        out_shape=jax.ShapeDtypeStruct((m, D_MODEL), BF16),
        compiler_params=_cparams(("parallel", "arbitrary")),
        name="merge",
    )(ya, yb, yc, u_g, u_g, u_g, w_branch)


def _mm_res_ln_kernel(x_ref, w_ref, res_ref, g_ref, b_ref, o_ref, ob_ref, acc_ref):
    kk = pl.program_id(1)

    @pl.when(kk == 0)
    def _():
        acc_ref[...] = jnp.zeros(acc_ref.shape, F32)

    acc_ref[...] += _dot(x_ref[...], w_ref[...])

    @pl.when(kk == pl.num_programs(1) - 1)
    def _():
        y = _ln_math(ALPHA * res_ref[...] + acc_ref[...], g_ref[...], b_ref[...])
        o_ref[...] = y
        ob_ref[...] = y.astype(BF16)


def matmul_res_ln(x, w, res, g, b, bk, name):
    m, k = x.shape
    n = w.shape[1]
    bm = _pick(m, 512)
    return pl.pallas_call(
        _mm_res_ln_kernel,
        grid=(m // bm, k // bk),
        in_specs=[pl.BlockSpec((bm, bk), lambda i, j: (i, j)),
                  pl.BlockSpec((bk, n), lambda i, j: (j, 0)),
                  pl.BlockSpec((bm, n), lambda i, j: (i, 0)),
                  pl.BlockSpec((1, n), lambda i, j: (0, 0)),
                  pl.BlockSpec((1, n), lambda i, j: (0, 0))],
        out_specs=[pl.BlockSpec((bm, n), lambda i, j: (i, 0)),
                   pl.BlockSpec((bm, n), lambda i, j: (i, 0))],
        out_shape=[jax.ShapeDtypeStruct((m, n), F32), jax.ShapeDtypeStruct((m, n), BF16)],
        scratch_shapes=[pltpu.VMEM((bm, n), F32)],
        compiler_params=_cparams(("parallel", "arbitrary")),
        name=name,
    )(x, w, res, g, b)


def _ffn_in_kernel(x_ref, wg_ref, wu_ref, o_ref):
    x = x_ref[...]
    gate = _dot(x, wg_ref[...])
    up = _dot(x, wu_ref[...])
    o_ref[...] = (gate * jax.nn.sigmoid(gate) * up).astype(o_ref.dtype)


def ffn_in(x, w):
    m, k = x.shape
    bm = _pick(m, 1024)
    bn = 512
    nj = D_FF // bn
    return pl.pallas_call(
        _ffn_in_kernel,
        grid=(m // bm, nj),
        in_specs=[pl.BlockSpec((bm, k), lambda i, j: (i, 0)),
                  pl.BlockSpec((k, bn), lambda i, j: (0, j)),
                  pl.BlockSpec((k, bn), lambda i, j: (0, nj + j))],
        out_specs=pl.BlockSpec((bm, bn), lambda i, j: (i, j)),
        out_shape=jax.ShapeDtypeStruct((m, D_FF), BF16),
        compiler_params=_cparams(("parallel", "arbitrary")),
        name="ffn_in",
    )(x, w, w)


def _prep_params(w_in, q_norm_g, w_uq, kv_norm_g, w_ukv, rwkv_mu, rwkv_w0, rwkv_w_up, rwkv_a0,
                 rwkv_a_up, rwkv_g_up, rwkv_k_k, rwkv_k_a, rwkv_r_k, rwkv_lnx_g, rwkv_lnx_b,
                 conv_w, w_branch, w_out, ln1_g, ln1_b, w_ffn_in, w_ffn_out, ln2_g, ln2_b):
    nl = w_in.shape[0]
    z = lambda *s: jnp.zeros(s, F32)
    o = MLA_COLS
    kr = w_in[:, :, o - ROPE:o]
    kr_rot = jnp.concatenate([-kr[..., ROPE // 2:], kr[..., :ROPE // 2]], axis=-1)
    w_mla = jnp.concatenate([w_in[:, :, :o], z(nl, D_MODEL, 64), kr_rot, z(nl, D_MODEL, 64)], axis=-1)
    w_rw = jnp.concatenate([w_in[:, :, o:o + RWKV_COLS], z(nl, D_MODEL, RWKV_PAD - RWKV_COLS)], axis=-1)
    o += RWKV_COLS
    w_cv = w_in[:, :, o:o + CONV_COLS]
    o += CONV_COLS
    w_gt = w_in[:, :, o:]

    wq = w_uq.reshape(nl, Q_LORA, MLA_HEADS, NOPE + ROPE)
    x1 = wq[..., NOPE:NOPE + ROPE // 2]
    x2 = wq[..., NOPE + ROPE // 2:]
    pad = z(nl, Q_LORA, MLA_HEADS, 128 - ROPE)
    wq_all = jnp.concatenate([wq[..., :NOPE].reshape(nl, Q_LORA, WIDTH),
                              jnp.concatenate([x1, x2, pad], -1).reshape(nl, Q_LORA, WIDTH),
                              jnp.concatenate([-x2, x1, pad], -1).reshape(nl, Q_LORA, WIDTH)], axis=-1)
    wkv = w_ukv.reshape(nl, KV_LORA, MLA_HEADS, NOPE + VDIM)
    wkv_all = jnp.concatenate([wkv[..., :NOPE].reshape(nl, KV_LORA, WIDTH),
                               wkv[..., NOPE:].reshape(nl, KV_LORA, WIDTH)], axis=-1)

    zl = z(nl, DECAY_LORA, WIDTH)
    wup = jnp.concatenate([jnp.concatenate([rwkv_w_up[:, 0], zl], -1),
                           jnp.concatenate([zl, rwkv_w_up[:, 1]], -1)], axis=1)
    aup = jnp.concatenate([jnp.concatenate([rwkv_a_up[:, 0], zl], -1),
                           jnp.concatenate([zl, rwkv_a_up[:, 1]], -1)], axis=1)
    gup = jnp.concatenate([rwkv_g_up, z(nl, GD_PAD - GATE_LORA, WIDTH)], axis=1)
    mu = jnp.concatenate([rwkv_mu, z(nl, 2, RWKV_PAD - RWKV_COLS)], axis=-1)
    head = jnp.arange(WIDTH) // RW_N
    ones_bd = (head[:, None] == head[None, :]).astype(BF16)
    row = lambda t: t.reshape(nl, 1, -1)
    return dict(
        w_mla=w_mla.astype(BF16), w_rw=w_rw.astype(BF16), w_cv=w_cv.astype(BF16), w_gt=w_gt.astype(BF16),
        qg=row(q_norm_g), kvg=row(kv_norm_g), wq=wq_all.astype(BF16), wkv=wkv_all.astype(BF16),
        mu=mu, w0=row(rwkv_w0), a0=row(rwkv_a0), wup=wup.astype(BF16), aup=aup.astype(BF16),
        gup=gup.astype(BF16), k_k=row(rwkv_k_k), k_a=row(rwkv_k_a), r_k=row(rwkv_r_k),
        lnx_g=row(rwkv_lnx_g), lnx_b=row(rwkv_lnx_b), conv_w=conv_w,
        w_branch=w_branch.astype(BF16), w_out=w_out.astype(BF16),
        ln1_g=row(ln1_g), ln1_b=row(ln1_b), w_ffn_in=w_ffn_in.astype(BF16),
        w_ffn_out=w_ffn_out.astype(BF16), ln2_g=row(ln2_g), ln2_b=row(ln2_b),
    ), ones_bd


def _rope_tables(seq):
    pos = jnp.arange(seq, dtype=F32)
    inv = ROPE_THETA ** (-jnp.arange(0, ROPE, 2, dtype=F32) / ROPE)
    ang = pos[:, None] * inv[None, :]
    c, s = jnp.cos(ang), jnp.sin(ang)
    zpad = jnp.zeros((seq, 128 - ROPE), F32)
    return jnp.concatenate([c, c, zpad], -1), jnp.concatenate([s, s, zpad], -1)


def _layer(x, xb, p, ones_bd, cosz, sinz, bsz, seq):
    u_a = matmul(xb, p["w_mla"], MLA_PAD, name="in_mla")
    u_b = matmul(xb, p["w_rw"], RWKV_PAD // 2, name="in_rwkv")
    u_c = matmul(xb, p["w_cv"], 1024, name="in_conv")
    u_g = matmul(xb, p["w_gt"], 1024, name="in_gate")

    q, k, v = mla_prep(u_a, p["qg"], p["kvg"], p["wq"], p["wkv"], cosz, sinz, seq)
    y_a = attention(q, k, v, bsz, seq)

    pp = dict(p, ones_bd=ones_bd)
    r, vv, a, lw0, lw1, k0, k1, b0, b1, bonus, g = rwkv_prep(u_b, pp, bsz, seq)
    y0 = rwkv_scan(r, lw0, k0, vv, a, b0, bsz, seq, reverse=False)
    y1 = rwkv_scan(r, lw1, k1, vv, a, b1, bsz, seq, reverse=True)
    y_b = rwkv_post(y0, y1, bonus, g, p["lnx_g"], p["lnx_b"], ones_bd)

    y_c = conv_branch(u_c, p["conv_w"], bsz, seq)

    merged = merge_branches(y_a, y_b, y_c, u_g, p["w_branch"])
    h, hb = matmul_res_ln(merged, p["w_out"], x, p["ln1_g"], p["ln1_b"], 512, "out_proj_ln")
    f = ffn_in(hb, p["w_ffn_in"])
    return matmul_res_ln(f, p["w_ffn_out"], h, p["ln2_g"], p["ln2_b"], 512, "ffn_out_ln")


def _trunk(x3, ln_g, ln_b, params, ones_bd):
    bsz, seq, d = x3.shape
    cosz, sinz = _rope_tables(seq)
    x, xb = layer_norm_in(x3.reshape(bsz * seq, d), ln_g.reshape(1, d), ln_b.reshape(1, d))
    for l in range(DEPTH):
        p = {name: t[l] for name, t in params.items()}
        x, xb = _layer(x, xb, p, ones_bd, cosz, sinz, bsz, seq)
    return x.reshape(bsz, seq, d)


def kernel(x_prompt, x_sample, ln_in_g, ln_in_b, w_in, q_norm_g, w_uq, kv_norm_g, w_ukv, rwkv_mu, rwkv_w0, rwkv_w_up, rwkv_a0, rwkv_a_up, rwkv_g_up, rwkv_k_k, rwkv_k_a, rwkv_r_k, rwkv_lnx_g, rwkv_lnx_b, conv_w, w_branch, w_out, ln1_g, ln1_b, w_ffn_in, w_ffn_out, ln2_g, ln2_b):
    params, ones_bd = _prep_params(w_in, q_norm_g, w_uq, kv_norm_g, w_ukv, rwkv_mu, rwkv_w0, rwkv_w_up,
                                   rwkv_a0, rwkv_a_up, rwkv_g_up, rwkv_k_k, rwkv_k_a, rwkv_r_k,
                                   rwkv_lnx_g, rwkv_lnx_b, conv_w, w_branch, w_out, ln1_g, ln1_b,
                                   w_ffn_in, w_ffn_out, ln2_g, ln2_b)
    y_prompt = _trunk(x_prompt, ln_in_g, ln_in_b, params, ones_bd)
    y_sample = _trunk(x_sample, ln_in_g, ln_in_b, params, ones_bd)
    return (y_prompt, y_sample)
```

```python
import functools

import jax
import jax.numpy as jnp
from jax import lax
from jax.experimental import pallas as pl
from jax.experimental.pallas import tpu as pltpu

F32 = jnp.float32
BF16 = jnp.bfloat16

D_MODEL = 2048
DEPTH = 4
WIDTH = D_MODEL // 2
MLA_HEADS = 8
NOPE = 128
ROPE = 64
VDIM = 128
Q_LORA = 768
KV_LORA = 512
ROPE_THETA = 10000.0
RW_HEADS = 16
RW_N = 64
DECAY_LORA = 64
AAA_LORA = 64
GATE_LORA = 160
D_FF = 5632
LN_EPS = 1e-5
RMS_EPS = 1e-6
LNX_EPS = 1e-5 * RW_N
ALPHA = (2 * DEPTH) ** 0.25
LOG2_E = 1.4426950408889634
MLA_COLS = Q_LORA + KV_LORA + ROPE
RWKV_COLS = 3 * WIDTH + 2 * DECAY_LORA + 2 * AAA_LORA + GATE_LORA
CONV_COLS = 3 * WIDTH
GATE_COLS = 3 * D_MODEL
MLA_PAD = 1536
RWKV_PAD = 3584
GD_PAD = 256
QK_PAD = 256
CHUNK = 64
CONV_HALO = 16

VMEM_LIMIT = 56 * 1024 * 1024


def _cparams(sem):
    return pltpu.CompilerParams(dimension_semantics=sem, vmem_limit_bytes=VMEM_LIMIT)


def _dot(a, b):
    return jnp.dot(a, b, preferred_element_type=F32)


def _dot_nt(a, b):
    return lax.dot_general(a, b, (((1,), (1,)), ((), ())), preferred_element_type=F32)


def _dot_tn(a, b):
    return lax.dot_general(a, b, (((0,), (0,)), ((), ())), preferred_element_type=F32)


def _split2(x):
    hi = x.astype(BF16)
    lo = (x - hi.astype(F32)).astype(BF16)
    return hi, lo


def _split3(x):
    hi = x.astype(BF16)
    r1 = x - hi.astype(F32)
    mid = r1.astype(BF16)
    lo = (r1 - mid.astype(F32)).astype(BF16)
    return hi, mid, lo


def _seg_sum(x, ones_bd):
    hi, lo = _split2(x)
    return _dot(hi, ones_bd) + _dot(lo, ones_bd)


def _pick(n, target):
    b = min(n, target)
    while n % b:
        b //= 2
    return b


def _ln_math(x, g, b):
    mu = jnp.mean(x, axis=-1, keepdims=True)
    xc = x - mu
    var = jnp.mean(xc * xc, axis=-1, keepdims=True)
    return xc * lax.rsqrt(var + LN_EPS) * g + b


def _ln_kernel(x_ref, g_ref, b_ref, o_ref, ob_ref):
    y = _ln_math(x_ref[...], g_ref[...], b_ref[...])
    o_ref[...] = y
    ob_ref[...] = y.astype(BF16)


def layer_norm_in(x, g, b):
    m, d = x.shape
    bm = _pick(m, 512)
    return pl.pallas_call(
        _ln_kernel,
        grid=(m // bm,),
        in_specs=[pl.BlockSpec((bm, d), lambda i: (i, 0)),
                  pl.BlockSpec((1, d), lambda i: (0, 0)),
                  pl.BlockSpec((1, d), lambda i: (0, 0))],
        out_specs=[pl.BlockSpec((bm, d), lambda i: (i, 0)),
                   pl.BlockSpec((bm, d), lambda i: (i, 0))],
        out_shape=[jax.ShapeDtypeStruct((m, d), F32), jax.ShapeDtypeStruct((m, d), BF16)],
        compiler_params=_cparams(("parallel",)),
        name="ln_in",
    )(x, g, b)


def _mm_kernel(x_ref, w_ref, o_ref, *, sigmoid):
    y = _dot(x_ref[...], w_ref[...])
    if sigmoid:
        y = jax.nn.sigmoid(y)
    o_ref[...] = y.astype(o_ref.dtype)


def matmul(x, w, bn, out_dtype=F32, name="mm", sigmoid=False):
    m, k = x.shape
    n = w.shape[1]
    bm = _pick(m, 1024)
    return pl.pallas_call(
        functools.partial(_mm_kernel, sigmoid=sigmoid),
        grid=(m // bm, n // bn),
        in_specs=[pl.BlockSpec((bm, k), lambda i, j: (i, 0)),
                  pl.BlockSpec((k, bn), lambda i, j: (0, j))],
        out_specs=pl.BlockSpec((bm, bn), lambda i, j: (i, j)),
        out_shape=jax.ShapeDtypeStruct((m, n), out_dtype),
        compiler_params=_cparams(("parallel", "arbitrary")),
        name=name,
    )(x, w)


def _rms(x, g):
    return x * lax.rsqrt(jnp.mean(x * x, axis=-1, keepdims=True) + RMS_EPS) * g


def _mla_prep_kernel(u_ref, qg_ref, kvg_ref, wq_ref, wkv_ref, cos_ref, sin_ref,
                     q_ref, k_ref, v_ref):
    scale = (NOPE + ROPE) ** -0.5 * LOG2_E
    cz = cos_ref[...]
    sz = sin_ref[...]
    cq = _rms(u_ref[:, :Q_LORA], qg_ref[...]).astype(BF16)
    pq = _dot(cq, wq_ref[...])
    ckv = _rms(u_ref[:, Q_LORA:Q_LORA + KV_LORA], kvg_ref[...]).astype(BF16)
    pkv = _dot(ckv, wkv_ref[...])
    o = Q_LORA + KV_LORA
    krz = (u_ref[:, o:o + 128] * cz + u_ref[:, o + 128:o + 256] * sz).astype(BF16)
    lane = lax.broadcasted_iota(jnp.int32, (cz.shape[0], 128), 1)
    one_col = jnp.where(lane == 0, 1.0, 0.0).astype(BF16)
    for h in range(MLA_HEADS):
        a, b = h * 128, (h + 1) * 128
        q_ref[:, h * QK_PAD:h * QK_PAD + 128] = (pq[:, a:b] * scale).astype(BF16)
        q_ref[:, h * QK_PAD + 128:(h + 1) * QK_PAD] = (
            (pq[:, WIDTH + a:WIDTH + b] * cz + pq[:, 2 * WIDTH + a:2 * WIDTH + b] * sz) * scale
        ).astype(BF16)
        k_ref[:, h * QK_PAD:h * QK_PAD + 128] = pkv[:, a:b].astype(BF16)
        k_ref[:, h * QK_PAD + 128:(h + 1) * QK_PAD] = krz
        v_ref[:, h * QK_PAD:h * QK_PAD + 128] = pkv[:, WIDTH + a:WIDTH + b].astype(BF16)
        v_ref[:, h * QK_PAD + 128:(h + 1) * QK_PAD] = one_col


def mla_prep(u_a, qg, kvg, wq, wkv, cosz, sinz, seq):
    m = u_a.shape[0]
    bm = _pick(seq, 512)
    nb = seq // bm
    full = lambda i: (0, 0)
    return pl.pallas_call(
        _mla_prep_kernel,
        grid=(m // bm,),
        in_specs=[pl.BlockSpec((bm, MLA_PAD), lambda i: (i, 0)),
                  pl.BlockSpec((1, Q_LORA), full),
                  pl.BlockSpec((1, KV_LORA), full),
                  pl.BlockSpec(wq.shape, full),
                  pl.BlockSpec(wkv.shape, full),
                  pl.BlockSpec((bm, 128), lambda i: (i % nb, 0)),
                  pl.BlockSpec((bm, 128), lambda i: (i % nb, 0))],
        out_specs=[pl.BlockSpec((bm, MLA_HEADS * QK_PAD), lambda i: (i, 0)),
                   pl.BlockSpec((bm, MLA_HEADS * QK_PAD), lambda i: (i, 0)),
                   pl.BlockSpec((bm, MLA_HEADS * QK_PAD), lambda i: (i, 0))],
        out_shape=[jax.ShapeDtypeStruct((m, MLA_HEADS * QK_PAD), BF16)] * 3,
        compiler_params=_cparams(("parallel",)),
        name="mla_prep",
    )(u_a, qg, kvg, wq, wkv, cosz, sinz)


def _attn_kernel(q_ref, k_ref, v_ref, o_ref, *, nkv, bk):
    q = q_ref[...]
    m = jnp.full((q.shape[0], 1), -jnp.inf, F32)
    acc = jnp.zeros((q.shape[0], QK_PAD), F32)
    s = _dot_nt(q, k_ref[0:bk, :])
    for j in range(nkv):
        if j + 1 < nkv:
            s_next = _dot_nt(q, k_ref[(j + 1) * bk:(j + 2) * bk, :])
        m_new = jnp.maximum(m, jnp.max(s, axis=-1, keepdims=True))
        alpha = jnp.exp2(m - m_new)
        p = jnp.exp2((s - m_new).astype(BF16))
        acc = alpha * acc + _dot(p, v_ref[j * bk:(j + 1) * bk, :])
        m = m_new
        if j + 1 < nkv:
            s = s_next
    o_ref[...] = (acc[:, :VDIM] / acc[:, VDIM:VDIM + 1]).astype(o_ref.dtype)


def attention(q, k, v, bsz, seq):
    m = q.shape[0]
    bq = _pick(seq, 1024)
    bk = _pick(seq, 1024)
    nq = seq // bq
    return pl.pallas_call(
        functools.partial(_attn_kernel, nkv=seq // bk, bk=bk),
        grid=(bsz, MLA_HEADS, nq),
        in_specs=[pl.BlockSpec((bq, QK_PAD), lambda b, h, i: (b * nq + i, h)),
                  pl.BlockSpec((seq, QK_PAD), lambda b, h, i: (b, h)),
                  pl.BlockSpec((seq, QK_PAD), lambda b, h, i: (b, h))],
        out_specs=pl.BlockSpec((bq, VDIM), lambda b, h, i: (b * nq + i, h)),
        out_shape=jax.ShapeDtypeStruct((m, WIDTH), BF16),
        compiler_params=_cparams(("parallel", "parallel", "arbitrary")),
        name="mla_attn",
    )(q, k, v)


def _shifted(u, prev_row, next_row):
    n = u.shape[0]
    row = lax.broadcasted_iota(jnp.int32, (n, 1), 0)
    up = jnp.where(row == 0, prev_row, pltpu.roll(u, 1, axis=0))
    un = jnp.where(row == n - 1, next_row, pltpu.roll(u, n - 1, axis=0))
    return up, un


def _halo_specs(bm, width, nb, bsz, hr):
    per = bm // hr
    last = bsz * nb * per - 1
    cur = pl.BlockSpec((bm, width), lambda b, i: (b * nb + i, 0))
    prev = pl.BlockSpec((hr, width), lambda b, i: (jnp.maximum((b * nb + i) * per - 1, 0), 0))
    nxt = pl.BlockSpec((hr, width), lambda b, i: (jnp.minimum((b * nb + i + 1) * per, last), 0))
    return cur, prev, nxt


def _rwkv_prep_kernel(u_ref, up_ref, un_ref, mu_ref, w0_ref, a0_ref, wup_ref, aup_ref, gup_ref,
                      kk_ref, ka_ref, rk_ref, ones_ref,
                      r_o, v_o, a_o, lw0_o, lw1_o, k0_o, k1_o, b0_o, b1_o, bonus_o, g_o):
    i = pl.program_id(1)
    u = u_ref[...]
    prev_row = jnp.where(i == 0, 0.0, up_ref[7:8, :])
    next_row = jnp.where(i == pl.num_programs(1) - 1, 0.0, un_ref[0:1, :])
    up, un = _shifted(u, prev_row, next_row)
    us = u + mu_ref[0:1, :] * (up - u) + mu_ref[1:2, :] * (un - u)
    r = us[:, 0:WIDTH]
    k = us[:, WIDTH:2 * WIDTH]
    v = us[:, 2 * WIDTH:3 * WIDTH]
    o = 3 * WIDTH
    wd = us[:, o:o + 2 * DECAY_LORA]
    ad = us[:, o + 128:o + 128 + 2 * AAA_LORA]
    gd = us[:, o + 256:o + 256 + GD_PAD]
    g_o[...] = _dot(jax.nn.sigmoid(gd).astype(BF16), gup_ref[...])
    z = -(w0_ref[...] + _dot(jnp.tanh(wd).astype(BF16), wup_ref[...]))
    softplus = jnp.maximum(z, 0.0) + jnp.log(1.0 + jnp.exp(-jnp.abs(z)))
    lw = -jnp.exp(-softplus - 0.5)
    al = jax.nn.sigmoid(a0_ref[...] + _dot(ad.astype(BF16), aup_ref[...]))
    ones_bd = ones_ref[...]
    kkr = k * kk_ref[...]
    nrm = jnp.sqrt(_seg_sum(kkr * kkr, ones_bd))
    kk = kkr / jnp.maximum(nrm, 1e-12)
    ka = ka_ref[...]
    a_0 = al[:, :WIDTH]
    a_1 = al[:, WIDTH:]
    k_0 = k * (1.0 + (a_0 - 1.0) * ka)
    k_1 = k * (1.0 + (a_1 - 1.0) * ka)
    r_o[...] = r.astype(r_o.dtype)
    v_o[...] = v.astype(v_o.dtype)
    a_o[...] = (-kk).astype(a_o.dtype)
    lw0_o[...] = lw[:, :WIDTH]
    lw1_o[...] = lw[:, WIDTH:]
    k0_o[...] = k_0.astype(k0_o.dtype)
    k1_o[...] = k_1.astype(k1_o.dtype)
    b0_o[...] = (kk * a_0).astype(b0_o.dtype)
    b1_o[...] = (kk * a_1).astype(b1_o.dtype)
    bonus_o[...] = _seg_sum(r * rk_ref[...] * (k_0 + k_1), ones_bd) * v


def rwkv_prep(u_b, p, bsz, seq):
    m = u_b.shape[0]
    bm = _pick(seq, 128)
    nb = seq // bm
    cur, prev, nxt = _halo_specs(bm, RWKV_PAD, nb, bsz, 8)
    full = lambda b, i: (0, 0)
    consts = [p["mu"], p["w0"], p["a0"], p["wup"], p["aup"], p["gup"], p["k_k"], p["k_a"], p["r_k"],
              p["ones_bd"]]
    out_spec = pl.BlockSpec((bm, WIDTH), lambda b, i: (b * nb + i, 0))
    return pl.pallas_call(
        _rwkv_prep_kernel,
        grid=(bsz, nb),
        in_specs=[cur, prev, nxt] + [pl.BlockSpec(c.shape, full) for c in consts],
        out_specs=[out_spec] * 11,
        out_shape=[jax.ShapeDtypeStruct((m, WIDTH), dt)
                   for dt in (BF16, BF16, BF16, F32, F32, BF16, BF16, BF16, BF16, F32, F32)],
        compiler_params=_cparams(("parallel", "arbitrary")),
        name="rwkv_prep",
    )(u_b, u_b, u_b, *consts)


def _unit_lower_inverse(a_heads):
    n = a_heads[0].shape[0]
    eye = (lax.broadcasted_iota(jnp.int32, (n, n), 0)
           == lax.broadcasted_iota(jnp.int32, (n, n), 1)).astype(F32)
    t = [a + eye for a in a_heads]
    pb = [a.astype(BF16) for a in a_heads]
    for _ in range(n.bit_length() - 2):
        pb = [_dot(x, x).astype(BF16) for x in pb]
        t = [ti + _dot(x, ti.astype(BF16)) for x, ti in zip(pb, t)]
    return t


def _scan_kernel(r_ref, lw_ref, k_ref, v_ref, a_ref, b_ref, y_ref, s_ref, *, reverse, nchunk):
    @pl.when(pl.program_id(1) == 0)
    def _():
        s_ref[...] = jnp.zeros(s_ref.shape, F32)

    ti = lax.broadcasted_iota(jnp.int32, (CHUNK, CHUNK), 0)
    si = lax.broadcasted_iota(jnp.int32, (CHUNK, CHUNK), 1)
    if reverse:
        incl, strict = si >= ti, si > ti
    else:
        incl, strict = si <= ti, si < ti
    tri = incl.astype(BF16)
    last = 0 if reverse else CHUNK - 1

    def chunk(ci, carry):
        c = (nchunk - 1 - ci) if reverse else ci
        rows = pl.ds(pl.multiple_of(c * CHUNK, CHUNK), CHUNK)
        lw = lw_ref[rows, :]
        l_hi, l_mid, l_lo = _split3(lw)
        cum = _dot(tri, l_hi) + _dot(tri, l_mid) + _dot(tri, l_lo)
        gam = jnp.exp(cum)
        ginv = jnp.exp(-cum)
        rt = r_ref[rows, :].astype(F32) * gam
        at = a_ref[rows, :].astype(F32) * jnp.exp(cum - lw)
        kt = k_ref[rows, :].astype(F32) * ginv
        bt = b_ref[rows, :].astype(F32) * ginv
        v = v_ref[rows, :].astype(F32)
        g_end = gam[last:last + 1, :]
        heads = range(RW_HEADS)
        sls = [slice(h * RW_N, (h + 1) * RW_N) for h in heads]
        bt_h = [bt[:, sl].astype(BF16) for sl in sls]
        kt_h = [kt[:, sl].astype(BF16) for sl in sls]
        v_h = [v[:, sl].astype(BF16) for sl in sls]
        ar = [jnp.concatenate([at[:, sl].astype(BF16), rt[:, sl].astype(BF16)], axis=0) for sl in sls]
        s0 = [s_ref[h] for h in heads]
        gb = [_dot_nt(ar[h], bt_h[h]) for h in heads]
        gk = [_dot_nt(ar[h], kt_h[h]) for h in heads]
        hs = [_dot_nt(ar[h], s0[h].astype(BF16)) for h in heads]
        a_ak = [jnp.where(strict, gk[h][:CHUNK], 0.0).astype(BF16) for h in heads]
        w_in = [hs[h][:CHUNK] + _dot(a_ak[h], v_h[h]) for h in heads]
        t_inv = _unit_lower_inverse([jnp.where(strict, gb[h][:CHUNK], 0.0) for h in heads])
        ub = [_dot(t_inv[h].astype(BF16), w_in[h].astype(BF16)).astype(BF16) for h in heads]
        a_rb = [jnp.where(incl, gb[h][CHUNK:], 0.0).astype(BF16) for h in heads]
        a_rk = [jnp.where(incl, gk[h][CHUNK:], 0.0).astype(BF16) for h in heads]
        y = [hs[h][CHUNK:] + _dot(a_rb[h], ub[h]) + _dot(a_rk[h], v_h[h]) for h in heads]
        ds = [_dot_tn(ub[h], bt_h[h]) + _dot_tn(v_h[h], kt_h[h]) for h in heads]
        for h in heads:
            s_ref[h] = (s0[h] + ds[h]) * g_end[:, sls[h]]
            y_ref[rows, sls[h]] = y[h]
        return carry

    lax.fori_loop(0, nchunk, chunk, 0)


def rwkv_scan(r, lw, k, v, a, b, bsz, seq, reverse):
    m = r.shape[0]
    bt = _pick(seq, 256)
    nb = seq // bt
    if reverse:
        imap = lambda bi, i: (bi * nb + nb - 1 - i, 0)
    else:
        imap = lambda bi, i: (bi * nb + i, 0)
    spec = pl.BlockSpec((bt, WIDTH), imap)
    return pl.pallas_call(
        functools.partial(_scan_kernel, reverse=reverse, nchunk=bt // CHUNK),
        grid=(bsz, nb),
        in_specs=[spec] * 6,
        out_specs=spec,
        out_shape=jax.ShapeDtypeStruct((m, WIDTH), F32),
        scratch_shapes=[pltpu.VMEM((RW_HEADS, RW_N, RW_N), F32)],
        compiler_params=_cparams(("parallel", "arbitrary")),
        name="rwkv_scan_bwd" if reverse else "rwkv_scan_fwd",
    )(r, lw, k, v, a, b)


def _rwkv_post_kernel(y0_ref, y1_ref, bonus_ref, g_ref, lg_ref, lb_ref, ones_ref, o_ref):
    ones_bd = ones_ref[...]
    y = y0_ref[...] + y1_ref[...]
    mu = _seg_sum(y, ones_bd) * (1.0 / RW_N)
    yc = y - mu
    var = _seg_sum(yc * yc, ones_bd) * (1.0 / RW_N)
    yn = yc * lax.rsqrt(var + LNX_EPS) * lg_ref[...] + lb_ref[...]
    o_ref[...] = ((yn + bonus_ref[...]) * g_ref[...]).astype(o_ref.dtype)


def rwkv_post(y0, y1, bonus, g, lg, lb, ones_bd):
    m = y0.shape[0]
    bm = _pick(m, 512)
    spec = pl.BlockSpec((bm, WIDTH), lambda i: (i, 0))
    full = lambda i: (0, 0)
    return pl.pallas_call(
        _rwkv_post_kernel,
        grid=(m // bm,),
        in_specs=[spec] * 4 + [pl.BlockSpec((1, WIDTH), full), pl.BlockSpec((1, WIDTH), full),
                               pl.BlockSpec((WIDTH, WIDTH), full)],
        out_specs=spec,
        out_shape=jax.ShapeDtypeStruct((m, WIDTH), BF16),
        compiler_params=_cparams(("parallel",)),
        name="rwkv_post",
    )(y0, y1, bonus, g, lg, lb, ones_bd)


def _conv_kernel(u_ref, up_ref, un_ref, w_ref, o_ref):
    i = pl.program_id(1)
    u = u_ref[...].astype(F32)
    hp = u[:, WIDTH:2 * WIDTH] * u[:, 2 * WIDTH:]
    pr = up_ref[...].astype(F32)[CONV_HALO - 1:CONV_HALO, :]
    nr = un_ref[...].astype(F32)[0:1, :]
    prev_row = jnp.where(i == 0, 0.0, pr[:, WIDTH:2 * WIDTH] * pr[:, 2 * WIDTH:])
    next_row = jnp.where(i == pl.num_programs(1) - 1, 0.0, nr[:, WIDTH:2 * WIDTH] * nr[:, 2 * WIDTH:])
    hm, hn = _shifted(hp, prev_row, next_row)
    conv = w_ref[0:1, :] * hm + w_ref[1:2, :] * hp + w_ref[2:3, :] * hn
    o_ref[...] = (u[:, :WIDTH] * conv).astype(o_ref.dtype)


def conv_branch(u_c, conv_w, bsz, seq):
    m = u_c.shape[0]
    bm = _pick(seq, 512)
    nb = seq // bm
    cur, prev, nxt = _halo_specs(bm, CONV_COLS, nb, bsz, CONV_HALO)
    return pl.pallas_call(
        _conv_kernel,
        grid=(bsz, nb),
        in_specs=[cur, prev, nxt, pl.BlockSpec((3, WIDTH), lambda b, i: (0, 0))],
        out_specs=pl.BlockSpec((bm, WIDTH), lambda b, i: (b * nb + i, 0)),
        out_shape=jax.ShapeDtypeStruct((m, WIDTH), BF16),
        compiler_params=_cparams(("parallel", "arbitrary")),
        name="conv_branch",
    )(u_c, u_c, u_c, conv_w)


def _merge_kernel(ya_ref, yb_ref, yc_ref, g0_ref, g1_ref, g2_ref, w_ref, o_ref):
    acc = g0_ref[...].astype(F32) * _dot(ya_ref[...], w_ref[0])
    acc = acc + g1_ref[...].astype(F32) * _dot(yb_ref[...], w_ref[1])
    acc = acc + g2_ref[...].astype(F32) * _dot(yc_ref[...], w_ref[2])
    o_ref[...] = acc.astype(o_ref.dtype)


def merge_branches(ya, yb, yc, gates, w_branch):
    m = ya.shape[0]
    bm = _pick(m, 1024)
    bn = 1024
    nj = D_MODEL // bn
    yspec = pl.BlockSpec((bm, WIDTH), lambda j, i: (i, 0))
    gspecs = [pl.BlockSpec((bm, bn), functools.partial(lambda j, i, t: (i, t * nj + j), t=t))
              for t in range(3)]
    return pl.pallas_call(
        _merge_kernel,
        grid=(nj, m // bm),
        in_specs=[yspec] * 3 + gspecs + [pl.BlockSpec((3, WIDTH, bn), lambda j, i: (0, 0, j))],
        out_specs=pl.BlockSpec((bm, bn), lambda j, i: (i, j)),
        out_shape=jax.ShapeDtypeStruct((m, D_MODEL), BF16),
        compiler_params=_cparams(("arbitrary", "arbitrary")),
        name="merge",
    )(ya, yb, yc, gates, gates, gates, w_branch)


def _out_proj_ln_kernel(x_ref, w_ref, res_ref, g_ref, b_ref, o_ref, ob_ref):
    y = _ln_math(ALPHA * res_ref[...] + _dot(x_ref[...], w_ref[...]), g_ref[...], b_ref[...])
    o_ref[...] = y
    ob_ref[...] = y.astype(BF16)


def out_proj_ln(x, w, res, g, b):
    m, k = x.shape
    n = w.shape[1]
    bm = _pick(m, 512)
    row = pl.BlockSpec((bm, n), lambda i: (i, 0))
    vec = pl.BlockSpec((1, n), lambda i: (0, 0))
    return pl.pallas_call(
        _out_proj_ln_kernel,
        grid=(m // bm,),
        in_specs=[pl.BlockSpec((bm, k), lambda i: (i, 0)), pl.BlockSpec((k, n), lambda i: (0, 0)),
                  row, vec, vec],
        out_specs=[row, row],
        out_shape=[jax.ShapeDtypeStruct((m, n), F32), jax.ShapeDtypeStruct((m, n), BF16)],
        compiler_params=_cparams(("parallel",)),
        name="out_proj_ln",
    )(x, w, res, g, b)


def _ffn_kernel(x_ref, wg_ref, wu_ref, wo_ref, res_ref, g_ref, b_ref, o_ref, ob_ref, acc_ref):
    j = pl.program_id(1)

    @pl.when(j == 0)
    def _():
        acc_ref[...] = jnp.zeros(acc_ref.shape, F32)

    x = x_ref[...]
    gate = _dot(x, wg_ref[...])
    up = _dot(x, wu_ref[...])
    f = (gate * jax.nn.sigmoid(gate) * up).astype(BF16)
    acc_ref[...] += _dot(f, wo_ref[...])

    @pl.when(j == pl.num_programs(1) - 1)
    def _():
        y = _ln_math(ALPHA * res_ref[...] + acc_ref[...], g_ref[...], b_ref[...])
        o_ref[...] = y
        ob_ref[...] = y.astype(BF16)


def ffn(x, w_in, w_out, res, g, b):
    m, k = x.shape
    n = w_out.shape[1]
    bm = _pick(m, 512)
    bn = 512
    nj = D_FF // bn
    row = pl.BlockSpec((bm, n), lambda i, j: (i, 0))
    vec = pl.BlockSpec((1, n), lambda i, j: (0, 0))
    return pl.pallas_call(
        _ffn_kernel,
        grid=(m // bm, nj),
        in_specs=[pl.BlockSpec((bm, k), lambda i, j: (i, 0)),
                  pl.BlockSpec((k, bn), lambda i, j: (0, j)),
                  pl.BlockSpec((k, bn), lambda i, j: (0, nj + j)),
                  pl.BlockSpec((bn, n), lambda i, j: (j, 0)),
                  row, vec, vec],
        out_specs=[row, row],
        out_shape=[jax.ShapeDtypeStruct((m, n), F32), jax.ShapeDtypeStruct((m, n), BF16)],
        scratch_shapes=[pltpu.VMEM((bm, n), F32)],
        compiler_params=_cparams(("parallel", "arbitrary")),
        name="ffn",
    )(x, w_in, w_in, w_out, res, g, b)


def _prep_params(w_in, q_norm_g, w_uq, kv_norm_g, w_ukv, rwkv_mu, rwkv_w0, rwkv_w_up, rwkv_a0,
                 rwkv_a_up, rwkv_g_up, rwkv_k_k, rwkv_k_a, rwkv_r_k, rwkv_lnx_g, rwkv_lnx_b,
                 conv_w, w_branch, w_out, ln1_g, ln1_b, w_ffn_in, w_ffn_out, ln2_g, ln2_b):
    nl = w_in.shape[0]
    z = lambda *s: jnp.zeros(s, F32)
    o = MLA_COLS
    kr = w_in[:, :, o - ROPE:o]
    kr_rot = jnp.concatenate([-kr[..., ROPE // 2:], kr[..., :ROPE // 2]], axis=-1)
    w_mla = jnp.concatenate([w_in[:, :, :o], z(nl, D_MODEL, 64), kr_rot, z(nl, D_MODEL, 64)], axis=-1)
    w_rw = jnp.concatenate([w_in[:, :, o:o + RWKV_COLS], z(nl, D_MODEL, RWKV_PAD - RWKV_COLS)], axis=-1)
    o += RWKV_COLS
    w_cv = w_in[:, :, o:o + CONV_COLS]
    o += CONV_COLS
    w_gt = w_in[:, :, o:]

    wq = w_uq.reshape(nl, Q_LORA, MLA_HEADS, NOPE + ROPE)
    x1 = wq[..., NOPE:NOPE + ROPE // 2]
    x2 = wq[..., NOPE + ROPE // 2:]
    pad = z(nl, Q_LORA, MLA_HEADS, 128 - ROPE)
    wq_all = jnp.concatenate([wq[..., :NOPE].reshape(nl, Q_LORA, WIDTH),
                              jnp.concatenate([x1, x2, pad], -1).reshape(nl, Q_LORA, WIDTH),
                              jnp.concatenate([-x2, x1, pad], -1).reshape(nl, Q_LORA, WIDTH)], axis=-1)
    wkv = w_ukv.reshape(nl, KV_LORA, MLA_HEADS, NOPE + VDIM)
    wkv_all = jnp.concatenate([wkv[..., :NOPE].reshape(nl, KV_LORA, WIDTH),
                               wkv[..., NOPE:].reshape(nl, KV_LORA, WIDTH)], axis=-1)

    zl = z(nl, DECAY_LORA, WIDTH)
    wup = jnp.concatenate([jnp.concatenate([rwkv_w_up[:, 0], zl], -1),
                           jnp.concatenate([zl, rwkv_w_up[:, 1]], -1)], axis=1)
    aup = jnp.concatenate([jnp.concatenate([rwkv_a_up[:, 0], zl], -1),
                           jnp.concatenate([zl, rwkv_a_up[:, 1]], -1)], axis=1)
    gup = jnp.concatenate([rwkv_g_up, z(nl, GD_PAD - GATE_LORA, WIDTH)], axis=1)
    mu = jnp.concatenate([rwkv_mu, z(nl, 2, RWKV_PAD - RWKV_COLS)], axis=-1)
    head = jnp.arange(WIDTH) // RW_N
    ones_bd = (head[:, None] == head[None, :]).astype(BF16)
    row = lambda t: t.reshape(nl, 1, -1)
    return dict(
        w_mla=w_mla.astype(BF16), w_rw=w_rw.astype(BF16), w_cv=w_cv.astype(BF16), w_gt=w_gt.astype(BF16),
        qg=row(q_norm_g), kvg=row(kv_norm_g), wq=wq_all.astype(BF16), wkv=wkv_all.astype(BF16),
        mu=mu, w0=row(rwkv_w0), a0=row(rwkv_a0), wup=wup.astype(BF16), aup=aup.astype(BF16),
        gup=gup.astype(BF16), k_k=row(rwkv_k_k), k_a=row(rwkv_k_a), r_k=row(rwkv_r_k),
        lnx_g=row(rwkv_lnx_g), lnx_b=row(rwkv_lnx_b), conv_w=conv_w,
        w_branch=w_branch.astype(BF16), w_out=w_out.astype(BF16),
        ln1_g=row(ln1_g), ln1_b=row(ln1_b), w_ffn_in=w_ffn_in.astype(BF16),
        w_ffn_out=w_ffn_out.astype(BF16), ln2_g=row(ln2_g), ln2_b=row(ln2_b),
    ), ones_bd


def _rope_tables(seq):
    pos = jnp.arange(seq, dtype=F32)
    inv = ROPE_THETA ** (-jnp.arange(0, ROPE, 2, dtype=F32) / ROPE)
    ang = pos[:, None] * inv[None, :]
    c, s = jnp.cos(ang), jnp.sin(ang)
    zpad = jnp.zeros((seq, 128 - ROPE), F32)
    return jnp.concatenate([c, c, zpad], -1), jnp.concatenate([s, s, zpad], -1)


def _layer(x, xb, p, ones_bd, cosz, sinz, bsz, seq):
    u_a = matmul(xb, p["w_mla"], MLA_PAD, name="in_mla")
    u_b = matmul(xb, p["w_rw"], RWKV_PAD // 2, name="in_rwkv")
    u_c = matmul(xb, p["w_cv"], 1024, out_dtype=BF16, name="in_conv")
    gates = matmul(xb, p["w_gt"], 1024, out_dtype=BF16, name="in_gate", sigmoid=True)

    q, k, v = mla_prep(u_a, p["qg"], p["kvg"], p["wq"], p["wkv"], cosz, sinz, seq)
    y_a = attention(q, k, v, bsz, seq)

    pp = dict(p, ones_bd=ones_bd)
    r, vv, a, lw0, lw1, k0, k1, b0, b1, bonus, g = rwkv_prep(u_b, pp, bsz, seq)
    y0 = rwkv_scan(r, lw0, k0, vv, a, b0, bsz, seq, reverse=False)
    y1 = rwkv_scan(r, lw1, k1, vv, a, b1, bsz, seq, reverse=True)
    y_b = rwkv_post(y0, y1, bonus, g, p["lnx_g"], p["lnx_b"], ones_bd)

    y_c = conv_branch(u_c, p["conv_w"], bsz, seq)

    merged = merge_branches(y_a, y_b, y_c, gates, p["w_branch"])
    h, hb = out_proj_ln(merged, p["w_out"], x, p["ln1_g"], p["ln1_b"])
    return ffn(hb, p["w_ffn_in"], p["w_ffn_out"], h, p["ln2_g"], p["ln2_b"])


def _trunk(x3, ln_g, ln_b, params, ones_bd):
    bsz, seq, d = x3.shape
    cosz, sinz = _rope_tables(seq)
    x, xb = layer_norm_in(x3.reshape(bsz * seq, d), ln_g.reshape(1, d), ln_b.reshape(1, d))
    for l in range(DEPTH):
        p = {name: t[l] for name, t in params.items()}
        x, xb = _layer(x, xb, p, ones_bd, cosz, sinz, bsz, seq)
    return x.reshape(bsz, seq, d)


def kernel(x_prompt, x_sample, ln_in_g, ln_in_b, w_in, q_norm_g, w_uq, kv_norm_g, w_ukv, rwkv_mu, rwkv_w0, rwkv_w_up, rwkv_a0, rwkv_a_up, rwkv_g_up, rwkv_k_k, rwkv_k_a, rwkv_r_k, rwkv_lnx_g, rwkv_lnx_b, conv_w, w_branch, w_out, ln1_g, ln1_b, w_ffn_in, w_ffn_out, ln2_g, ln2_b):
    params, ones_bd = _prep_params(w_in, q_norm_g, w_uq, kv_norm_g, w_ukv, rwkv_mu, rwkv_w0, rwkv_w_up,
                                   rwkv_a0, rwkv_a_up, rwkv_g_up, rwkv_k_k, rwkv_k_a, rwkv_r_k,
                                   rwkv_lnx_g, rwkv_lnx_b, conv_w, w_branch, w_out, ln1_g, ln1_b,
                                   w_ffn_in, w_ffn_out, ln2_g, ln2_b)
    y_prompt = _trunk(x_prompt, ln_in_g, ln_in_b, params, ones_bd)
    y_sample = _trunk(x_sample, ln_in_g, ln_in_b, params, ones_bd)
    return (y_prompt, y_sample)
```

```python
import functools

import jax
import jax.numpy as jnp
from jax import lax
from jax.experimental import pallas as pl
from jax.experimental.pallas import tpu as pltpu

F32 = jnp.float32
BF16 = jnp.bfloat16

D_MODEL = 2048
DEPTH = 4
WIDTH = D_MODEL // 2
MLA_HEADS = 8
NOPE = 128
ROPE = 64
VDIM = 128
Q_LORA = 768
KV_LORA = 512
ROPE_THETA = 10000.0
RW_HEADS = 16
RW_N = 64
DECAY_LORA = 64
AAA_LORA = 64
GATE_LORA = 160
D_FF = 5632
LN_EPS = 1e-5
RMS_EPS = 1e-6
LNX_EPS = 1e-5 * RW_N
ALPHA = (2 * DEPTH) ** 0.25
LOG2_E = 1.4426950408889634
MLA_COLS = Q_LORA + KV_LORA + ROPE
RWKV_COLS = 3 * WIDTH + 2 * DECAY_LORA + 2 * AAA_LORA + GATE_LORA
CONV_COLS = 3 * WIDTH
GATE_COLS = 3 * D_MODEL
MLA_PAD = 1536
RWKV_PAD = 3584
GD_PAD = 256
QK_PAD = 256
CHUNK = 64
CONV_HALO = 16

VMEM_LIMIT = 56 * 1024 * 1024


def _cparams(sem):
    return pltpu.CompilerParams(dimension_semantics=sem, vmem_limit_bytes=VMEM_LIMIT)


def _dot(a, b):
    return jnp.dot(a, b, preferred_element_type=F32)


def _dot_nt(a, b):
    return lax.dot_general(a, b, (((1,), (1,)), ((), ())), preferred_element_type=F32)


def _dot_tn(a, b):
    return lax.dot_general(a, b, (((0,), (0,)), ((), ())), preferred_element_type=F32)


def _split2(x):
    hi = x.astype(BF16)
    lo = (x - hi.astype(F32)).astype(BF16)
    return hi, lo


def _split3(x):
    hi = x.astype(BF16)
    r1 = x - hi.astype(F32)
    mid = r1.astype(BF16)
    lo = (r1 - mid.astype(F32)).astype(BF16)
    return hi, mid, lo


def _seg_sum(x, ones_bd):
    hi, lo = _split2(x)
    return _dot(hi, ones_bd) + _dot(lo, ones_bd)


def _pick(n, target):
    b = min(n, target)
    while n % b:
        b //= 2
    return b


def _ln_math(x, g, b):
    mu = jnp.mean(x, axis=-1, keepdims=True)
    xc = x - mu
    var = jnp.mean(xc * xc, axis=-1, keepdims=True)
    return xc * lax.rsqrt(var + LN_EPS) * g + b


def _ln_kernel(x_ref, g_ref, b_ref, o_ref, ob_ref):
    y = _ln_math(x_ref[...], g_ref[...], b_ref[...])
    o_ref[...] = y
    ob_ref[...] = y.astype(BF16)


def layer_norm_in(x, g, b):
    m, d = x.shape
    bm = _pick(m, 512)
    return pl.pallas_call(
        _ln_kernel,
        grid=(m // bm,),
        in_specs=[pl.BlockSpec((bm, d), lambda i: (i, 0)),
                  pl.BlockSpec((1, d), lambda i: (0, 0)),
                  pl.BlockSpec((1, d), lambda i: (0, 0))],
        out_specs=[pl.BlockSpec((bm, d), lambda i: (i, 0)),
                   pl.BlockSpec((bm, d), lambda i: (i, 0))],
        out_shape=[jax.ShapeDtypeStruct((m, d), F32), jax.ShapeDtypeStruct((m, d), BF16)],
        compiler_params=_cparams(("parallel",)),
        name="ln_in",
    )(x, g, b)


def _mm_kernel(x_ref, w_ref, o_ref, *, sigmoid):
    y = _dot(x_ref[...], w_ref[...])
    if sigmoid:
        y = jax.nn.sigmoid(y)
    o_ref[...] = y.astype(o_ref.dtype)


def matmul(x, w, bn, out_dtype=F32, name="mm", sigmoid=False):
    m, k = x.shape
    n = w.shape[1]
    bm = _pick(m, 1024)
    return pl.pallas_call(
        functools.partial(_mm_kernel, sigmoid=sigmoid),
        grid=(m // bm, n // bn),
        in_specs=[pl.BlockSpec((bm, k), lambda i, j: (i, 0)),
                  pl.BlockSpec((k, bn), lambda i, j: (0, j))],
        out_specs=pl.BlockSpec((bm, bn), lambda i, j: (i, j)),
        out_shape=jax.ShapeDtypeStruct((m, n), out_dtype),
        compiler_params=_cparams(("parallel", "arbitrary")),
        name=name,
    )(x, w)


def _rms(x, g):
    return x * lax.rsqrt(jnp.mean(x * x, axis=-1, keepdims=True) + RMS_EPS) * g


def _mla_prep_kernel(u_ref, qg_ref, kvg_ref, wq_ref, wkv_ref, cos_ref, sin_ref,
                     q_ref, k_ref, v_ref):
    scale = (NOPE + ROPE) ** -0.5 * LOG2_E
    cz = cos_ref[...]
    sz = sin_ref[...]
    cq = _rms(u_ref[:, :Q_LORA], qg_ref[...]).astype(BF16)
    pq = _dot(cq, wq_ref[...])
    ckv = _rms(u_ref[:, Q_LORA:Q_LORA + KV_LORA], kvg_ref[...]).astype(BF16)
    pkv = _dot(ckv, wkv_ref[...])
    o = Q_LORA + KV_LORA
    krz = (u_ref[:, o:o + 128] * cz + u_ref[:, o + 128:o + 256] * sz).astype(BF16)
    lane = lax.broadcasted_iota(jnp.int32, (cz.shape[0], 128), 1)
    one_col = jnp.where(lane == 0, 1.0, 0.0).astype(BF16)
    for h in range(MLA_HEADS):
        a, b = h * 128, (h + 1) * 128
        q_ref[:, h * QK_PAD:h * QK_PAD + 128] = (pq[:, a:b] * scale).astype(BF16)
        q_ref[:, h * QK_PAD + 128:(h + 1) * QK_PAD] = (
            (pq[:, WIDTH + a:WIDTH + b] * cz + pq[:, 2 * WIDTH + a:2 * WIDTH + b] * sz) * scale
        ).astype(BF16)
        k_ref[:, h * QK_PAD:h * QK_PAD + 128] = pkv[:, a:b].astype(BF16)
        k_ref[:, h * QK_PAD + 128:(h + 1) * QK_PAD] = krz
        v_ref[:, h * QK_PAD:h * QK_PAD + 128] = pkv[:, WIDTH + a:WIDTH + b].astype(BF16)
        v_ref[:, h * QK_PAD + 128:(h + 1) * QK_PAD] = one_col


def mla_prep(u_a, qg, kvg, wq, wkv, cosz, sinz, seq):
    m = u_a.shape[0]
    bm = _pick(seq, 512)
    nb = seq // bm
    full = lambda i: (0, 0)
    return pl.pallas_call(
        _mla_prep_kernel,
        grid=(m // bm,),
        in_specs=[pl.BlockSpec((bm, MLA_PAD), lambda i: (i, 0)),
                  pl.BlockSpec((1, Q_LORA), full),
                  pl.BlockSpec((1, KV_LORA), full),
                  pl.BlockSpec(wq.shape, full),
                  pl.BlockSpec(wkv.shape, full),
                  pl.BlockSpec((bm, 128), lambda i: (i % nb, 0)),
                  pl.BlockSpec((bm, 128), lambda i: (i % nb, 0))],
        out_specs=[pl.BlockSpec((bm, MLA_HEADS * QK_PAD), lambda i: (i, 0)),
                   pl.BlockSpec((bm, MLA_HEADS * QK_PAD), lambda i: (i, 0)),
                   pl.BlockSpec((bm, MLA_HEADS * QK_PAD), lambda i: (i, 0))],
        out_shape=[jax.ShapeDtypeStruct((m, MLA_HEADS * QK_PAD), BF16)] * 3,
        compiler_params=_cparams(("parallel",)),
        name="mla_prep",
    )(u_a, qg, kvg, wq, wkv, cosz, sinz)


def _attn_kernel(q_ref, k_ref, v_ref, o_ref, *, nkv, bk):
    q = q_ref[...]
    m = jnp.full((q.shape[0], 1), -jnp.inf, F32)
    acc = jnp.zeros((q.shape[0], QK_PAD), F32)
    s = _dot_nt(q, k_ref[0:bk, :])
    for j in range(nkv):
        if j + 1 < nkv:
            s_next = _dot_nt(q, k_ref[(j + 1) * bk:(j + 2) * bk, :])
        m_new = jnp.maximum(m, jnp.max(s, axis=-1, keepdims=True))
        alpha = jnp.exp2(m - m_new)
        p = jnp.exp2((s - m_new).astype(BF16))
        acc = alpha * acc + _dot(p, v_ref[j * bk:(j + 1) * bk, :])
        m = m_new
        if j + 1 < nkv:
            s = s_next
    o_ref[...] = (acc[:, :VDIM] / acc[:, VDIM:VDIM + 1]).astype(o_ref.dtype)


def attention(q, k, v, bsz, seq):
    m = q.shape[0]
    bq = _pick(seq, 1024)
    bk = _pick(seq, 1024)
    nq = seq // bq
    return pl.pallas_call(
        functools.partial(_attn_kernel, nkv=seq // bk, bk=bk),
        grid=(bsz, MLA_HEADS, nq),
        in_specs=[pl.BlockSpec((bq, QK_PAD), lambda b, h, i: (b * nq + i, h)),
                  pl.BlockSpec((seq, QK_PAD), lambda b, h, i: (b, h)),
                  pl.BlockSpec((seq, QK_PAD), lambda b, h, i: (b, h))],
        out_specs=pl.BlockSpec((bq, VDIM), lambda b, h, i: (b * nq + i, h)),
        out_shape=jax.ShapeDtypeStruct((m, WIDTH), BF16),
        compiler_params=_cparams(("parallel", "parallel", "arbitrary")),
        name="mla_attn",
    )(q, k, v)


def _shifted(u, prev_row, next_row):
    n = u.shape[0]
    row = lax.broadcasted_iota(jnp.int32, (n, 1), 0)
    up = jnp.where(row == 0, prev_row, pltpu.roll(u, 1, axis=0))
    un = jnp.where(row == n - 1, next_row, pltpu.roll(u, n - 1, axis=0))
    return up, un


def _halo_specs(bm, width, nb, bsz, hr):
    per = bm // hr
    last = bsz * nb * per - 1
    cur = pl.BlockSpec((bm, width), lambda b, i: (b * nb + i, 0))
    prev = pl.BlockSpec((hr, width), lambda b, i: (jnp.maximum((b * nb + i) * per - 1, 0), 0))
    nxt = pl.BlockSpec((hr, width), lambda b, i: (jnp.minimum((b * nb + i + 1) * per, last), 0))
    return cur, prev, nxt


def _rwkv_prep_kernel(u_ref, up_ref, un_ref, mu_ref, w0_ref, a0_ref, wup_ref, aup_ref, gup_ref,
                      kk_ref, ka_ref, rk_ref, ones_ref,
                      r_o, v_o, a_o, lw0_o, lw1_o, k0_o, k1_o, b0_o, b1_o, bonus_o, g_o):
    i = pl.program_id(1)
    u = u_ref[...]
    prev_row = jnp.where(i == 0, 0.0, up_ref[7:8, :])
    next_row = jnp.where(i == pl.num_programs(1) - 1, 0.0, un_ref[0:1, :])
    up, un = _shifted(u, prev_row, next_row)
    us = u + mu_ref[0:1, :] * (up - u) + mu_ref[1:2, :] * (un - u)
    r = us[:, 0:WIDTH]
    k = us[:, WIDTH:2 * WIDTH]
    v = us[:, 2 * WIDTH:3 * WIDTH]
    o = 3 * WIDTH
    wd = us[:, o:o + 2 * DECAY_LORA]
    ad = us[:, o + 128:o + 128 + 2 * AAA_LORA]
    gd = us[:, o + 256:o + 256 + GD_PAD]
    g_o[...] = _dot(jax.nn.sigmoid(gd).astype(BF16), gup_ref[...])
    z = -(w0_ref[...] + _dot(jnp.tanh(wd).astype(BF16), wup_ref[...]))
    softplus = jnp.maximum(z, 0.0) + jnp.log(1.0 + jnp.exp(-jnp.abs(z)))
    lw = -jnp.exp(-softplus - 0.5)
    al = jax.nn.sigmoid(a0_ref[...] + _dot(ad.astype(BF16), aup_ref[...]))
    ones_bd = ones_ref[...]
    kkr = k * kk_ref[...]
    nrm = jnp.sqrt(_seg_sum(kkr * kkr, ones_bd))
    kk = kkr / jnp.maximum(nrm, 1e-12)
    ka = ka_ref[...]
    a_0 = al[:, :WIDTH]
    a_1 = al[:, WIDTH:]
    k_0 = k * (1.0 + (a_0 - 1.0) * ka)
    k_1 = k * (1.0 + (a_1 - 1.0) * ka)
    r_o[...] = r.astype(r_o.dtype)
    v_o[...] = v.astype(v_o.dtype)
    a_o[...] = (-kk).astype(a_o.dtype)
    lw0_o[...] = lw[:, :WIDTH]
    lw1_o[...] = lw[:, WIDTH:]
    k0_o[...] = k_0.astype(k0_o.dtype)
    k1_o[...] = k_1.astype(k1_o.dtype)
    b0_o[...] = (kk * a_0).astype(b0_o.dtype)
    b1_o[...] = (kk * a_1).astype(b1_o.dtype)
    bonus_o[...] = _seg_sum(r * rk_ref[...] * (k_0 + k_1), ones_bd) * v


def rwkv_prep(u_b, p, bsz, seq):
    m = u_b.shape[0]
    bm = _pick(seq, 128)
    nb = seq // bm
    cur, prev, nxt = _halo_specs(bm, RWKV_PAD, nb, bsz, 8)
    full = lambda b, i: (0, 0)
    consts = [p["mu"], p["w0"], p["a0"], p["wup"], p["aup"], p["gup"], p["k_k"], p["k_a"], p["r_k"],
              p["ones_bd"]]
    out_spec = pl.BlockSpec((bm, WIDTH), lambda b, i: (b * nb + i, 0))
    return pl.pallas_call(
        _rwkv_prep_kernel,
        grid=(bsz, nb),
        in_specs=[cur, prev, nxt] + [pl.BlockSpec(c.shape, full) for c in consts],
        out_specs=[out_spec] * 11,
        out_shape=[jax.ShapeDtypeStruct((m, WIDTH), dt)
                   for dt in (BF16, BF16, BF16, F32, F32, BF16, BF16, BF16, BF16, F32, F32)],
        compiler_params=_cparams(("parallel", "arbitrary")),
        name="rwkv_prep",
    )(u_b, u_b, u_b, *consts)


def _scan_chunk_operands(refs, rows, reverse):
    r_ref, lw_ref, k_ref, v_ref, a_ref, b_ref = refs
    ti = lax.broadcasted_iota(jnp.int32, (CHUNK, CHUNK), 0)
    si = lax.broadcasted_iota(jnp.int32, (CHUNK, CHUNK), 1)
    tri = ((si >= ti) if reverse else (si <= ti)).astype(BF16)
    last = 0 if reverse else CHUNK - 1
    lw = lw_ref[rows, :]
    l_hi, l_mid, l_lo = _split3(lw)
    cum = _dot(tri, l_hi) + _dot(tri, l_mid) + _dot(tri, l_lo)
    gam = jnp.exp(cum)
    ginv = jnp.exp(-cum)
    rt = r_ref[rows, :].astype(F32) * gam
    at = a_ref[rows, :].astype(F32) * jnp.exp(cum - lw)
    kt = k_ref[rows, :].astype(F32) * ginv
    bt = b_ref[rows, :].astype(F32) * ginv
    v = v_ref[rows, :].astype(F32)
    g_end = gam[last:last + 1, :]
    out = []
    for h in range(RW_HEADS):
        sl = slice(h * RW_N, (h + 1) * RW_N)
        ar = jnp.concatenate([at[:, sl].astype(BF16), rt[:, sl].astype(BF16)], axis=0)
        bk = jnp.concatenate([bt[:, sl].astype(BF16), kt[:, sl].astype(BF16)], axis=0)
        out.append((ar, bk, v[:, sl].astype(BF16), g_end[:, sl]))
    return out


def _scan_kernel(rf_ref, lwf_ref, kf_ref, vf_ref, af_ref, bf_ref,
                 rb_ref, lwb_ref, kb_ref, vb_ref, ab_ref, bb_ref,
                 yf_ref, yb_ref, s_ref, *, nchunk):
    @pl.when(pl.program_id(1) == 0)
    def _():
        s_ref[...] = jnp.zeros(s_ref.shape, F32)

    ti = lax.broadcasted_iota(jnp.int32, (CHUNK, 2 * CHUNK), 0)
    lane = lax.broadcasted_iota(jnp.int32, (CHUNK, 2 * CHUNK), 1)
    si = lane & (CHUNK - 1)
    left = lane < CHUNK
    eye_right = jnp.where(lane == ti + CHUNK, 1.0, 0.0)
    masks = []
    for reverse in (False, True):
        strict = (si > ti) if reverse else (si < ti)
        incl = (si >= ti) if reverse else (si <= ti)
        masks += [(strict, incl)] * RW_HEADS
    n = 2 * RW_HEADS
    streams = range(n)

    def chunk(ci, carry):
        rows_f = pl.ds(pl.multiple_of(ci * CHUNK, CHUNK), CHUNK)
        rows_b = pl.ds(pl.multiple_of((nchunk - 1 - ci) * CHUNK, CHUNK), CHUNK)
        ops = (_scan_chunk_operands((rf_ref, lwf_ref, kf_ref, vf_ref, af_ref, bf_ref), rows_f, False)
               + _scan_chunk_operands((rb_ref, lwb_ref, kb_ref, vb_ref, ab_ref, bb_ref), rows_b, True))
        ar = [o[0] for o in ops]
        bk = [o[1] for o in ops]
        v_h = [o[2] for o in ops]
        g_end = [o[3] for o in ops]
        s0 = [s_ref[i] for i in streams]
        g = [_dot_nt(ar[i], bk[i]) for i in streams]
        hs = [_dot_nt(ar[i], s0[i].astype(BF16)) for i in streams]
        g_top = [jnp.where(masks[i][0], g[i][:CHUNK], 0.0) for i in streams]
        g_bot = [jnp.where(masks[i][1], g[i][CHUNK:], 0.0).astype(BF16) for i in streams]
        vv = [jnp.concatenate([v_h[i], v_h[i]], axis=0) for i in streams]
        w_in = [hs[i][:CHUNK] + _dot(jnp.where(left, 0.0, g_top[i]).astype(BF16), vv[i]) for i in streams]
        z = [jnp.where(left, g_top[i], eye_right) for i in streams]
        for _ in range(CHUNK.bit_length() - 1):
            r = [_dot(z[i][:, :CHUNK].astype(BF16), z[i].astype(BF16)) for i in streams]
            z = [r[i] + jnp.where(left, 0.0, z[i]) for i in streams]
        t_inv = [jnp.where(left, 0.0, z[i]).astype(BF16) for i in streams]
        ww = [jnp.concatenate([w_in[i], w_in[i]], axis=0).astype(BF16) for i in streams]
        uv = [jnp.concatenate([_dot(t_inv[i], ww[i]).astype(BF16), v_h[i]], axis=0) for i in streams]
        y = [hs[i][CHUNK:] + _dot(g_bot[i], uv[i]) for i in streams]
        ds = [_dot_tn(uv[i], bk[i]) for i in streams]
        for i in streams:
            s_ref[i] = (s0[i] + ds[i]) * g_end[i]
            h = i % RW_HEADS
            sl = slice(h * RW_N, (h + 1) * RW_N)
            if i < RW_HEADS:
                yf_ref[rows_f, sl] = y[i]
            else:
                yb_ref[rows_b, sl] = y[i]
        return carry

    lax.fori_loop(0, nchunk, chunk, 0)


def rwkv_scan(r, v, a, lw0, k0, b0, lw1, k1, b1, bsz, seq):
    m = r.shape[0]
    bt = _pick(seq, 256)
    nb = seq // bt
    fwd = pl.BlockSpec((bt, WIDTH), lambda bi, i: (bi * nb + i, 0))
    bwd = pl.BlockSpec((bt, WIDTH), lambda bi, i: (bi * nb + nb - 1 - i, 0))
    return pl.pallas_call(
        functools.partial(_scan_kernel, nchunk=bt // CHUNK),
        grid=(bsz, nb),
        in_specs=[fwd] * 6 + [bwd] * 6,
        out_specs=[fwd, bwd],
        out_shape=[jax.ShapeDtypeStruct((m, WIDTH), F32)] * 2,
        scratch_shapes=[pltpu.VMEM((2 * RW_HEADS, RW_N, RW_N), F32)],
        compiler_params=_cparams(("parallel", "arbitrary")),
        name="rwkv_scan",
    )(r, lw0, k0, v, a, b0, r, lw1, k1, v, a, b1)


def _rwkv_post_kernel(y0_ref, y1_ref, bonus_ref, g_ref, lg_ref, lb_ref, ones_ref, o_ref):
    ones_bd = ones_ref[...]
    y = y0_ref[...] + y1_ref[...]
    mu = _seg_sum(y, ones_bd) * (1.0 / RW_N)
    yc = y - mu
    var = _seg_sum(yc * yc, ones_bd) * (1.0 / RW_N)
    yn = yc * lax.rsqrt(var + LNX_EPS) * lg_ref[...] + lb_ref[...]
    o_ref[...] = ((yn + bonus_ref[...]) * g_ref[...]).astype(o_ref.dtype)


def rwkv_post(y0, y1, bonus, g, lg, lb, ones_bd):
    m = y0.shape[0]
    bm = _pick(m, 512)
    spec = pl.BlockSpec((bm, WIDTH), lambda i: (i, 0))
    full = lambda i: (0, 0)
    return pl.pallas_call(
        _rwkv_post_kernel,
        grid=(m // bm,),
        in_specs=[spec] * 4 + [pl.BlockSpec((1, WIDTH), full), pl.BlockSpec((1, WIDTH), full),
                               pl.BlockSpec((WIDTH, WIDTH), full)],
        out_specs=spec,
        out_shape=jax.ShapeDtypeStruct((m, WIDTH), BF16),
        compiler_params=_cparams(("parallel",)),
        name="rwkv_post",
    )(y0, y1, bonus, g, lg, lb, ones_bd)


def _conv_kernel(u_ref, up_ref, un_ref, w_ref, o_ref):
    i = pl.program_id(1)
    u = u_ref[...].astype(F32)
    hp = u[:, WIDTH:2 * WIDTH] * u[:, 2 * WIDTH:]
    pr = up_ref[...].astype(F32)[CONV_HALO - 1:CONV_HALO, :]
    nr = un_ref[...].astype(F32)[0:1, :]
    prev_row = jnp.where(i == 0, 0.0, pr[:, WIDTH:2 * WIDTH] * pr[:, 2 * WIDTH:])
    next_row = jnp.where(i == pl.num_programs(1) - 1, 0.0, nr[:, WIDTH:2 * WIDTH] * nr[:, 2 * WIDTH:])
    hm, hn = _shifted(hp, prev_row, next_row)
    conv = w_ref[0:1, :] * hm + w_ref[1:2, :] * hp + w_ref[2:3, :] * hn
    o_ref[...] = (u[:, :WIDTH] * conv).astype(o_ref.dtype)


def conv_branch(u_c, conv_w, bsz, seq):
    m = u_c.shape[0]
    bm = _pick(seq, 512)
    nb = seq // bm
    cur, prev, nxt = _halo_specs(bm, CONV_COLS, nb, bsz, CONV_HALO)
    return pl.pallas_call(
        _conv_kernel,
        grid=(bsz, nb),
        in_specs=[cur, prev, nxt, pl.BlockSpec((3, WIDTH), lambda b, i: (0, 0))],
        out_specs=pl.BlockSpec((bm, WIDTH), lambda b, i: (b * nb + i, 0)),
        out_shape=jax.ShapeDtypeStruct((m, WIDTH), BF16),
        compiler_params=_cparams(("parallel", "arbitrary")),
        name="conv_branch",
    )(u_c, u_c, u_c, conv_w)


def _merge_kernel(ya_ref, yb_ref, yc_ref, g0_ref, g1_ref, g2_ref, w_ref, o_ref):
    acc = g0_ref[...].astype(F32) * _dot(ya_ref[...], w_ref[0])
    acc = acc + g1_ref[...].astype(F32) * _dot(yb_ref[...], w_ref[1])
    acc = acc + g2_ref[...].astype(F32) * _dot(yc_ref[...], w_ref[2])
    o_ref[...] = acc.astype(o_ref.dtype)


def merge_branches(ya, yb, yc, gates, w_branch):
    m = ya.shape[0]
    bm = _pick(m, 1024)
    bn = 1024
    nj = D_MODEL // bn
    yspec = pl.BlockSpec((bm, WIDTH), lambda j, i: (i, 0))
    gspecs = [pl.BlockSpec((bm, bn), functools.partial(lambda j, i, t: (i, t * nj + j), t=t))
              for t in range(3)]
    return pl.pallas_call(
        _merge_kernel,
        grid=(nj, m // bm),
        in_specs=[yspec] * 3 + gspecs + [pl.BlockSpec((3, WIDTH, bn), lambda j, i: (0, 0, j))],
        out_specs=pl.BlockSpec((bm, bn), lambda j, i: (i, j)),
        out_shape=jax.ShapeDtypeStruct((m, D_MODEL), BF16),
        compiler_params=_cparams(("arbitrary", "arbitrary")),
        name="merge",
    )(ya, yb, yc, gates, gates, gates, w_branch)


def _out_proj_ln_kernel(x_ref, w_ref, res_ref, g_ref, b_ref, o_ref, ob_ref):
    y = _ln_math(ALPHA * res_ref[...] + _dot(x_ref[...], w_ref[...]), g_ref[...], b_ref[...])
    o_ref[...] = y
    ob_ref[...] = y.astype(BF16)


def out_proj_ln(x, w, res, g, b):
    m, k = x.shape
    n = w.shape[1]
    bm = _pick(m, 512)
    row = pl.BlockSpec((bm, n), lambda i: (i, 0))
    vec = pl.BlockSpec((1, n), lambda i: (0, 0))
    return pl.pallas_call(
        _out_proj_ln_kernel,
        grid=(m // bm,),
        in_specs=[pl.BlockSpec((bm, k), lambda i: (i, 0)), pl.BlockSpec((k, n), lambda i: (0, 0)),
                  row, vec, vec],
        out_specs=[row, row],
        out_shape=[jax.ShapeDtypeStruct((m, n), F32), jax.ShapeDtypeStruct((m, n), BF16)],
        compiler_params=_cparams(("parallel",)),
        name="out_proj_ln",
    )(x, w, res, g, b)


def _ffn_kernel(x_ref, wg_ref, wu_ref, wo_ref, res_ref, g_ref, b_ref, o_ref, ob_ref, acc_ref):
    j = pl.program_id(1)

    @pl.when(j == 0)
    def _():
        acc_ref[...] = jnp.zeros(acc_ref.shape, F32)

    x = x_ref[...]
    gate = _dot(x, wg_ref[...])
    up = _dot(x, wu_ref[...])
    f = (gate * jax.nn.sigmoid(gate) * up).astype(BF16)
    acc_ref[...] += _dot(f, wo_ref[...])

    @pl.when(j == pl.num_programs(1) - 1)
    def _():
        y = _ln_math(ALPHA * res_ref[...] + acc_ref[...], g_ref[...], b_ref[...])
        o_ref[...] = y
        ob_ref[...] = y.astype(BF16)


def ffn(x, w_in, w_out, res, g, b):
    m, k = x.shape
    n = w_out.shape[1]
    bm = _pick(m, 512)
    bn = 512
    nj = D_FF // bn
    row = pl.BlockSpec((bm, n), lambda i, j: (i, 0))
    vec = pl.BlockSpec((1, n), lambda i, j: (0, 0))
    return pl.pallas_call(
        _ffn_kernel,
        grid=(m // bm, nj),
        in_specs=[pl.BlockSpec((bm, k), lambda i, j: (i, 0)),
                  pl.BlockSpec((k, bn), lambda i, j: (0, j)),
                  pl.BlockSpec((k, bn), lambda i, j: (0, nj + j)),
                  pl.BlockSpec((bn, n), lambda i, j: (j, 0)),
                  row, vec, vec],
        out_specs=[row, row],
        out_shape=[jax.ShapeDtypeStruct((m, n), F32), jax.ShapeDtypeStruct((m, n), BF16)],
        scratch_shapes=[pltpu.VMEM((bm, n), F32)],
        compiler_params=_cparams(("parallel", "arbitrary")),
        name="ffn",
    )(x, w_in, w_in, w_out, res, g, b)


def _prep_params(w_in, q_norm_g, w_uq, kv_norm_g, w_ukv, rwkv_mu, rwkv_w0, rwkv_w_up, rwkv_a0,
                 rwkv_a_up, rwkv_g_up, rwkv_k_k, rwkv_k_a, rwkv_r_k, rwkv_lnx_g, rwkv_lnx_b,
                 conv_w, w_branch, w_out, ln1_g, ln1_b, w_ffn_in, w_ffn_out, ln2_g, ln2_b):
    nl = w_in.shape[0]
    z = lambda *s: jnp.zeros(s, F32)
    o = MLA_COLS
    kr = w_in[:, :, o - ROPE:o]
    kr_rot = jnp.concatenate([-kr[..., ROPE // 2:], kr[..., :ROPE // 2]], axis=-1)
    w_mla = jnp.concatenate([w_in[:, :, :o], z(nl, D_MODEL, 64), kr_rot, z(nl, D_MODEL, 64)], axis=-1)
    w_rw = jnp.concatenate([w_in[:, :, o:o + RWKV_COLS], z(nl, D_MODEL, RWKV_PAD - RWKV_COLS)], axis=-1)
    o += RWKV_COLS
    w_cv = w_in[:, :, o:o + CONV_COLS]
    o += CONV_COLS
    w_gt = w_in[:, :, o:]

    wq = w_uq.reshape(nl, Q_LORA, MLA_HEADS, NOPE + ROPE)
    x1 = wq[..., NOPE:NOPE + ROPE // 2]
    x2 = wq[..., NOPE + ROPE // 2:]
    pad = z(nl, Q_LORA, MLA_HEADS, 128 - ROPE)
    wq_all = jnp.concatenate([wq[..., :NOPE].reshape(nl, Q_LORA, WIDTH),
                              jnp.concatenate([x1, x2, pad], -1).reshape(nl, Q_LORA, WIDTH),
                              jnp.concatenate([-x2, x1, pad], -1).reshape(nl, Q_LORA, WIDTH)], axis=-1)
    wkv = w_ukv.reshape(nl, KV_LORA, MLA_HEADS, NOPE + VDIM)
    wkv_all = jnp.concatenate([wkv[..., :NOPE].reshape(nl, KV_LORA, WIDTH),
                               wkv[..., NOPE:].reshape(nl, KV_LORA, WIDTH)], axis=-1)

    zl = z(nl, DECAY_LORA, WIDTH)
    wup = jnp.concatenate([jnp.concatenate([rwkv_w_up[:, 0], zl], -1),
                           jnp.concatenate([zl, rwkv_w_up[:, 1]], -1)], axis=1)
    aup = jnp.concatenate([jnp.concatenate([rwkv_a_up[:, 0], zl], -1),
                           jnp.concatenate([zl, rwkv_a_up[:, 1]], -1)], axis=1)
    gup = jnp.concatenate([rwkv_g_up, z(nl, GD_PAD - GATE_LORA, WIDTH)], axis=1)
    mu = jnp.concatenate([rwkv_mu, z(nl, 2, RWKV_PAD - RWKV_COLS)], axis=-1)
    head = jnp.arange(WIDTH) // RW_N
    ones_bd = (head[:, None] == head[None, :]).astype(BF16)
    row = lambda t: t.reshape(nl, 1, -1)
    return dict(
        w_mla=w_mla.astype(BF16), w_rw=w_rw.astype(BF16), w_cv=w_cv.astype(BF16), w_gt=w_gt.astype(BF16),
        qg=row(q_norm_g), kvg=row(kv_norm_g), wq=wq_all.astype(BF16), wkv=wkv_all.astype(BF16),
        mu=mu, w0=row(rwkv_w0), a0=row(rwkv_a0), wup=wup.astype(BF16), aup=aup.astype(BF16),
        gup=gup.astype(BF16), k_k=row(rwkv_k_k), k_a=row(rwkv_k_a), r_k=row(rwkv_r_k),
        lnx_g=row(rwkv_lnx_g), lnx_b=row(rwkv_lnx_b), conv_w=conv_w,
        w_branch=w_branch.astype(BF16), w_out=w_out.astype(BF16),
        ln1_g=row(ln1_g), ln1_b=row(ln1_b), w_ffn_in=w_ffn_in.astype(BF16),
        w_ffn_out=w_ffn_out.astype(BF16), ln2_g=row(ln2_g), ln2_b=row(ln2_b),
    ), ones_bd


def _rope_tables(seq):
    pos = jnp.arange(seq, dtype=F32)
    inv = ROPE_THETA ** (-jnp.arange(0, ROPE, 2, dtype=F32) / ROPE)
    ang = pos[:, None] * inv[None, :]
    c, s = jnp.cos(ang), jnp.sin(ang)
    zpad = jnp.zeros((seq, 128 - ROPE), F32)
    return jnp.concatenate([c, c, zpad], -1), jnp.concatenate([s, s, zpad], -1)


def _layer(x, xb, p, ones_bd, cosz, sinz, bsz, seq):
    u_a = matmul(xb, p["w_mla"], MLA_PAD, name="in_mla")
    u_b = matmul(xb, p["w_rw"], RWKV_PAD // 2, name="in_rwkv")
    u_c = matmul(xb, p["w_cv"], 1024, out_dtype=BF16, name="in_conv")
    gates = matmul(xb, p["w_gt"], 1024, out_dtype=BF16, name="in_gate", sigmoid=True)

    q, k, v = mla_prep(u_a, p["qg"], p["kvg"], p["wq"], p["wkv"], cosz, sinz, seq)
    y_a = attention(q, k, v, bsz, seq)

    pp = dict(p, ones_bd=ones_bd)
    r, vv, a, lw0, lw1, k0, k1, b0, b1, bonus, g = rwkv_prep(u_b, pp, bsz, seq)
    y0, y1 = rwkv_scan(r, vv, a, lw0, k0, b0, lw1, k1, b1, bsz, seq)
    y_b = rwkv_post(y0, y1, bonus, g, p["lnx_g"], p["lnx_b"], ones_bd)

    y_c = conv_branch(u_c, p["conv_w"], bsz, seq)

    merged = merge_branches(y_a, y_b, y_c, gates, p["w_branch"])
    h, hb = out_proj_ln(merged, p["w_out"], x, p["ln1_g"], p["ln1_b"])
    return ffn(hb, p["w_ffn_in"], p["w_ffn_out"], h, p["ln2_g"], p["ln2_b"])


def _trunk(x3, ln_g, ln_b, params, ones_bd):
    bsz, seq, d = x3.shape
    cosz, sinz = _rope_tables(seq)
    x, xb = layer_norm_in(x3.reshape(bsz * seq, d), ln_g.reshape(1, d), ln_b.reshape(1, d))
    for l in range(DEPTH):
        p = {name: t[l] for name, t in params.items()}
        x, xb = _layer(x, xb, p, ones_bd, cosz, sinz, bsz, seq)
    return x.reshape(bsz, seq, d)


def kernel(x_prompt, x_sample, ln_in_g, ln_in_b, w_in, q_norm_g, w_uq, kv_norm_g, w_ukv, rwkv_mu, rwkv_w0, rwkv_w_up, rwkv_a0, rwkv_a_up, rwkv_g_up, rwkv_k_k, rwkv_k_a, rwkv_r_k, rwkv_lnx_g, rwkv_lnx_b, conv_w, w_branch, w_out, ln1_g, ln1_b, w_ffn_in, w_ffn_out, ln2_g, ln2_b):
    params, ones_bd = _prep_params(w_in, q_norm_g, w_uq, kv_norm_g, w_ukv, rwkv_mu, rwkv_w0, rwkv_w_up,
                                   rwkv_a0, rwkv_a_up, rwkv_g_up, rwkv_k_k, rwkv_k_a, rwkv_r_k,
                                   rwkv_lnx_g, rwkv_lnx_b, conv_w, w_branch, w_out, ln1_g, ln1_b,
                                   w_ffn_in, w_ffn_out, ln2_g, ln2_b)
    y_prompt = _trunk(x_prompt, ln_in_g, ln_in_b, params, ones_bd)
    y_sample = _trunk(x_sample, ln_in_g, ln_in_b, params, ones_bd)
    return (y_prompt, y_sample)
```

```python
import functools

import jax
import jax.numpy as jnp
from jax import lax
from jax.experimental import pallas as pl
from jax.experimental.pallas import tpu as pltpu

F32 = jnp.float32
BF16 = jnp.bfloat16

D_MODEL = 2048
DEPTH = 4
WIDTH = D_MODEL // 2
MLA_HEADS = 8
NOPE = 128
ROPE = 64
VDIM = 128
Q_LORA = 768
KV_LORA = 512
ROPE_THETA = 10000.0
RW_HEADS = 16
RW_N = 64
DECAY_LORA = 64
AAA_LORA = 64
GATE_LORA = 160
D_FF = 5632
LN_EPS = 1e-5
RMS_EPS = 1e-6
LNX_EPS = 1e-5 * RW_N
ALPHA = (2 * DEPTH) ** 0.25
LOG2_E = 1.4426950408889634
MLA_COLS = Q_LORA + KV_LORA + ROPE
RWKV_COLS = 3 * WIDTH + 2 * DECAY_LORA + 2 * AAA_LORA + GATE_LORA
CONV_COLS = 3 * WIDTH
GATE_COLS = 3 * D_MODEL
MLA_PAD = 1536
RWKV_PAD = 3584
GD_PAD = 256
QK_PAD = 256
CHUNK = 64
CONV_HALO = 16
SEG_LANES = 256

VMEM_LIMIT = 56 * 1024 * 1024


def _cparams(sem):
    return pltpu.CompilerParams(dimension_semantics=sem, vmem_limit_bytes=VMEM_LIMIT)


def _dot(a, b):
    return jnp.dot(a, b, preferred_element_type=F32)


def _dot_nt(a, b):
    return lax.dot_general(a, b, (((1,), (1,)), ((), ())), preferred_element_type=F32)


def _dot_tn(a, b):
    return lax.dot_general(a, b, (((0,), (0,)), ((), ())), preferred_element_type=F32)


def _split2(x):
    hi = x.astype(BF16)
    lo = (x - hi.astype(F32)).astype(BF16)
    return hi, lo


def _split3(x):
    hi = x.astype(BF16)
    r1 = x - hi.astype(F32)
    mid = r1.astype(BF16)
    lo = (r1 - mid.astype(F32)).astype(BF16)
    return hi, mid, lo


def _seg_sum(x, ones_bd):
    hi, lo = _split2(x)
    parts = []
    for c in range(0, x.shape[1], SEG_LANES):
        parts.append(_dot(hi[:, c:c + SEG_LANES], ones_bd) + _dot(lo[:, c:c + SEG_LANES], ones_bd))
    return jnp.concatenate(parts, axis=1)


def _pick(n, target):
    b = min(n, target)
    while n % b:
        b //= 2
    return b


def _ln_math(x, g, b):
    mu = jnp.mean(x, axis=-1, keepdims=True)
    xc = x - mu
    var = jnp.mean(xc * xc, axis=-1, keepdims=True)
    return xc * lax.rsqrt(var + LN_EPS) * g + b


def _ln_kernel(x_ref, g_ref, b_ref, o_ref, ob_ref):
    y = _ln_math(x_ref[...], g_ref[...], b_ref[...])
    o_ref[...] = y
    ob_ref[...] = y.astype(BF16)


def layer_norm_in(x, g, b):
    m, d = x.shape
    bm = _pick(m, 512)
    return pl.pallas_call(
        _ln_kernel,
        grid=(m // bm,),
        in_specs=[pl.BlockSpec((bm, d), lambda i: (i, 0)),
                  pl.BlockSpec((1, d), lambda i: (0, 0)),
                  pl.BlockSpec((1, d), lambda i: (0, 0))],
        out_specs=[pl.BlockSpec((bm, d), lambda i: (i, 0)),
                   pl.BlockSpec((bm, d), lambda i: (i, 0))],
        out_shape=[jax.ShapeDtypeStruct((m, d), F32), jax.ShapeDtypeStruct((m, d), BF16)],
        compiler_params=_cparams(("parallel",)),
        name="ln_in",
    )(x, g, b)


def _mm_kernel(x_ref, w_ref, o_ref, *, sigmoid):
    y = _dot(x_ref[...], w_ref[...])
    if sigmoid:
        y = jax.nn.sigmoid(y)
    o_ref[...] = y.astype(o_ref.dtype)


def matmul(x, w, bn, out_dtype=F32, name="mm", sigmoid=False):
    m, k = x.shape
    n = w.shape[1]
    bm = _pick(m, 1024)
    return pl.pallas_call(
        functools.partial(_mm_kernel, sigmoid=sigmoid),
        grid=(m // bm, n // bn),
        in_specs=[pl.BlockSpec((bm, k), lambda i, j: (i, 0)),
                  pl.BlockSpec((k, bn), lambda i, j: (0, j))],
        out_specs=pl.BlockSpec((bm, bn), lambda i, j: (i, j)),
        out_shape=jax.ShapeDtypeStruct((m, n), out_dtype),
        compiler_params=_cparams(("parallel", "arbitrary")),
        name=name,
    )(x, w)


def _rms(x, g):
    return x * lax.rsqrt(jnp.mean(x * x, axis=-1, keepdims=True) + RMS_EPS) * g


def _mla_prep_kernel(u_ref, qg_ref, kvg_ref, wq_ref, wkv_ref, cos_ref, sin_ref,
                     q_ref, k_ref, v_ref):
    scale = (NOPE + ROPE) ** -0.5 * LOG2_E
    cz = cos_ref[...]
    sz = sin_ref[...]
    cq = _rms(u_ref[:, :Q_LORA], qg_ref[...]).astype(BF16)
    pq = _dot(cq, wq_ref[...])
    ckv = _rms(u_ref[:, Q_LORA:Q_LORA + KV_LORA], kvg_ref[...]).astype(BF16)
    pkv = _dot(ckv, wkv_ref[...])
    o = Q_LORA + KV_LORA
    krz = (u_ref[:, o:o + 128] * cz + u_ref[:, o + 128:o + 256] * sz).astype(BF16)
    lane = lax.broadcasted_iota(jnp.int32, (cz.shape[0], 128), 1)
    one_col = jnp.where(lane == 0, 1.0, 0.0).astype(BF16)
    for h in range(MLA_HEADS):
        a, b = h * 128, (h + 1) * 128
        q_ref[:, h * QK_PAD:h * QK_PAD + 128] = (pq[:, a:b] * scale).astype(BF16)
        q_ref[:, h * QK_PAD + 128:(h + 1) * QK_PAD] = (
            (pq[:, WIDTH + a:WIDTH + b] * cz + pq[:, 2 * WIDTH + a:2 * WIDTH + b] * sz) * scale
        ).astype(BF16)
        k_ref[:, h * QK_PAD:h * QK_PAD + 128] = pkv[:, a:b].astype(BF16)
        k_ref[:, h * QK_PAD + 128:(h + 1) * QK_PAD] = krz
        v_ref[:, h * QK_PAD:h * QK_PAD + 128] = pkv[:, WIDTH + a:WIDTH + b].astype(BF16)
        v_ref[:, h * QK_PAD + 128:(h + 1) * QK_PAD] = one_col


def mla_prep(u_a, qg, kvg, wq, wkv, cosz, sinz, seq):
    m = u_a.shape[0]
    bm = _pick(seq, 512)
    nb = seq // bm
    full = lambda i: (0, 0)
    return pl.pallas_call(
        _mla_prep_kernel,
        grid=(m // bm,),
        in_specs=[pl.BlockSpec((bm, MLA_PAD), lambda i: (i, 0)),
                  pl.BlockSpec((1, Q_LORA), full),
                  pl.BlockSpec((1, KV_LORA), full),
                  pl.BlockSpec(wq.shape, full),
                  pl.BlockSpec(wkv.shape, full),
                  pl.BlockSpec((bm, 128), lambda i: (i % nb, 0)),
                  pl.BlockSpec((bm, 128), lambda i: (i % nb, 0))],
        out_specs=[pl.BlockSpec((bm, MLA_HEADS * QK_PAD), lambda i: (i, 0)),
                   pl.BlockSpec((bm, MLA_HEADS * QK_PAD), lambda i: (i, 0)),
                   pl.BlockSpec((bm, MLA_HEADS * QK_PAD), lambda i: (i, 0))],
        out_shape=[jax.ShapeDtypeStruct((m, MLA_HEADS * QK_PAD), BF16)] * 3,
        compiler_params=_cparams(("parallel",)),
        name="mla_prep",
    )(u_a, qg, kvg, wq, wkv, cosz, sinz)


def _attn_kernel(q_ref, k_ref, v_ref, o_ref, *, nkv, bk):
    q = q_ref[...]
    m = jnp.full((q.shape[0], 1), -jnp.inf, F32)
    acc = jnp.zeros((q.shape[0], QK_PAD), F32)
    s = _dot_nt(q, k_ref[0:bk, :])
    for j in range(nkv):
        if j + 1 < nkv:
            s_next = _dot_nt(q, k_ref[(j + 1) * bk:(j + 2) * bk, :])
        m_new = jnp.maximum(m, jnp.max(s, axis=-1, keepdims=True))
        alpha = jnp.exp2(m - m_new)
        p = jnp.exp2((s - m_new).astype(BF16))
        acc = alpha * acc + _dot(p, v_ref[j * bk:(j + 1) * bk, :])
        m = m_new
        if j + 1 < nkv:
            s = s_next
    o_ref[...] = (acc[:, :VDIM] / acc[:, VDIM:VDIM + 1]).astype(o_ref.dtype)


def attention(q, k, v, bsz, seq):
    m = q.shape[0]
    bq = _pick(seq, 1024)
    bk = _pick(seq, 1024)
    nq = seq // bq
    return pl.pallas_call(
        functools.partial(_attn_kernel, nkv=seq // bk, bk=bk),
        grid=(bsz, MLA_HEADS, nq),
        in_specs=[pl.BlockSpec((bq, QK_PAD), lambda b, h, i: (b * nq + i, h)),
                  pl.BlockSpec((seq, QK_PAD), lambda b, h, i: (b, h)),
                  pl.BlockSpec((seq, QK_PAD), lambda b, h, i: (b, h))],
        out_specs=pl.BlockSpec((bq, VDIM), lambda b, h, i: (b * nq + i, h)),
        out_shape=jax.ShapeDtypeStruct((m, WIDTH), BF16),
        compiler_params=_cparams(("parallel", "parallel", "arbitrary")),
        name="mla_attn",
    )(q, k, v)


def _shifted(u, prev_row, next_row):
    n = u.shape[0]
    row = lax.broadcasted_iota(jnp.int32, (n, 1), 0)
    up = jnp.where(row == 0, prev_row, pltpu.roll(u, 1, axis=0))
    un = jnp.where(row == n - 1, next_row, pltpu.roll(u, n - 1, axis=0))
    return up, un


def _halo_specs(bm, width, nb, bsz, hr):
    per = bm // hr
    last = bsz * nb * per - 1
    cur = pl.BlockSpec((bm, width), lambda b, i: (b * nb + i, 0))
    prev = pl.BlockSpec((hr, width), lambda b, i: (jnp.maximum((b * nb + i) * per - 1, 0), 0))
    nxt = pl.BlockSpec((hr, width), lambda b, i: (jnp.minimum((b * nb + i + 1) * per, last), 0))
    return cur, prev, nxt


def _rwkv_prep_kernel(u_ref, up_ref, un_ref, mu_ref, w0_ref, a0_ref, wup_ref, aup_ref, gup_ref,
                      kk_ref, ka_ref, rk_ref, ones_ref,
                      r_o, v_o, a_o, lw0_o, lw1_o, k0_o, k1_o, b0_o, b1_o, bonus_o, g_o):
    i = pl.program_id(1)
    u = u_ref[...]
    prev_row = jnp.where(i == 0, 0.0, up_ref[7:8, :])
    next_row = jnp.where(i == pl.num_programs(1) - 1, 0.0, un_ref[0:1, :])
    up, un = _shifted(u, prev_row, next_row)
    us = u + mu_ref[0:1, :] * (up - u) + mu_ref[1:2, :] * (un - u)
    r = us[:, 0:WIDTH]
    k = us[:, WIDTH:2 * WIDTH]
    v = us[:, 2 * WIDTH:3 * WIDTH]
    o = 3 * WIDTH
    wd = us[:, o:o + 2 * DECAY_LORA]
    ad = us[:, o + 128:o + 128 + 2 * AAA_LORA]
    gd = us[:, o + 256:o + 256 + GD_PAD]
    g_o[...] = _dot(jax.nn.sigmoid(gd).astype(BF16), gup_ref[...])
    z = -(w0_ref[...] + _dot(jnp.tanh(wd).astype(BF16), wup_ref[...]))
    softplus = jnp.maximum(z, 0.0) + jnp.log(1.0 + jnp.exp(-jnp.abs(z)))
    lw = -jnp.exp(-softplus - 0.5)
    al = jax.nn.sigmoid(a0_ref[...] + _dot(ad.astype(BF16), aup_ref[...]))
    ones_bd = ones_ref[...]
    kkr = k * kk_ref[...]
    nrm = jnp.sqrt(_seg_sum(kkr * kkr, ones_bd))
    kk = kkr / jnp.maximum(nrm, 1e-12)
    ka = ka_ref[...]
    a_0 = al[:, :WIDTH]
    a_1 = al[:, WIDTH:]
    k_0 = k * (1.0 + (a_0 - 1.0) * ka)
    k_1 = k * (1.0 + (a_1 - 1.0) * ka)
    r_o[...] = r.astype(r_o.dtype)
    v_o[...] = v.astype(v_o.dtype)
    a_o[...] = (-kk).astype(a_o.dtype)
    lw0_o[...] = lw[:, :WIDTH]
    lw1_o[...] = lw[:, WIDTH:]
    k0_o[...] = k_0.astype(k0_o.dtype)
    k1_o[...] = k_1.astype(k1_o.dtype)
    b0_o[...] = (kk * a_0).astype(b0_o.dtype)
    b1_o[...] = (kk * a_1).astype(b1_o.dtype)
    bonus_o[...] = _seg_sum(r * rk_ref[...] * (k_0 + k_1), ones_bd) * v


def rwkv_prep(u_b, p, bsz, seq):
    m = u_b.shape[0]
    bm = _pick(seq, 128)
    nb = seq // bm
    cur, prev, nxt = _halo_specs(bm, RWKV_PAD, nb, bsz, 8)
    full = lambda b, i: (0, 0)
    consts = [p["mu"], p["w0"], p["a0"], p["wup"], p["aup"], p["gup"], p["k_k"], p["k_a"], p["r_k"],
              p["ones_bd"]]
    out_spec = pl.BlockSpec((bm, WIDTH), lambda b, i: (b * nb + i, 0))
    return pl.pallas_call(
        _rwkv_prep_kernel,
        grid=(bsz, nb),
        in_specs=[cur, prev, nxt] + [pl.BlockSpec(c.shape, full) for c in consts],
        out_specs=[out_spec] * 11,
        out_shape=[jax.ShapeDtypeStruct((m, WIDTH), dt)
                   for dt in (BF16, BF16, BF16, F32, F32, BF16, BF16, BF16, BF16, F32, F32)],
        compiler_params=_cparams(("parallel", "arbitrary")),
        name="rwkv_prep",
    )(u_b, u_b, u_b, *consts)


def _scan_chunk_operands(refs, rows, reverse):
    r_ref, lw_ref, k_ref, v_ref, a_ref, b_ref = refs
    ti = lax.broadcasted_iota(jnp.int32, (CHUNK, CHUNK), 0)
    si = lax.broadcasted_iota(jnp.int32, (CHUNK, CHUNK), 1)
    tri = ((si >= ti) if reverse else (si <= ti)).astype(BF16)
    last = 0 if reverse else CHUNK - 1
    lw = lw_ref[rows, :]
    l_hi, l_mid, l_lo = _split3(lw)
    cum = _dot(tri, l_hi) + _dot(tri, l_mid) + _dot(tri, l_lo)
    gam = jnp.exp(cum)
    ginv = jnp.exp(-cum)
    rt = r_ref[rows, :].astype(F32) * gam
    at = a_ref[rows, :].astype(F32) * jnp.exp(cum - lw)
    kt = k_ref[rows, :].astype(F32) * ginv
    bt = b_ref[rows, :].astype(F32) * ginv
    v = v_ref[rows, :].astype(F32)
    g_end = gam[last:last + 1, :]
    out = []
    for h in range(RW_HEADS):
        sl = slice(h * RW_N, (h + 1) * RW_N)
        ar = jnp.concatenate([at[:, sl].astype(BF16), rt[:, sl].astype(BF16)], axis=0)
        bk = jnp.concatenate([bt[:, sl].astype(BF16), kt[:, sl].astype(BF16)], axis=0)
        out.append((ar, bk, v[:, sl].astype(BF16), g_end[:, sl]))
    return out


def _scan_kernel(rf_ref, lwf_ref, kf_ref, vf_ref, af_ref, bf_ref,
                 rb_ref, lwb_ref, kb_ref, vb_ref, ab_ref, bb_ref,
                 yf_ref, yb_ref, s_ref, *, nchunk):
    @pl.when(pl.program_id(1) == 0)
    def _():
        s_ref[...] = jnp.zeros(s_ref.shape, F32)

    ti = lax.broadcasted_iota(jnp.int32, (CHUNK, 2 * CHUNK), 0)
    lane = lax.broadcasted_iota(jnp.int32, (CHUNK, 2 * CHUNK), 1)
    si = lane & (CHUNK - 1)
    left = lane < CHUNK
    eye_right = jnp.where(lane == ti + CHUNK, 1.0, 0.0)
    masks = []
    for reverse in (False, True):
        strict = (si > ti) if reverse else (si < ti)
        incl = (si >= ti) if reverse else (si <= ti)
        masks += [(strict, incl)] * RW_HEADS
    n = 2 * RW_HEADS
    streams = range(n)

    def chunk(ci, carry):
        rows_f = pl.ds(pl.multiple_of(ci * CHUNK, CHUNK), CHUNK)
        rows_b = pl.ds(pl.multiple_of((nchunk - 1 - ci) * CHUNK, CHUNK), CHUNK)
        ops = (_scan_chunk_operands((rf_ref, lwf_ref, kf_ref, vf_ref, af_ref, bf_ref), rows_f, False)
               + _scan_chunk_operands((rb_ref, lwb_ref, kb_ref, vb_ref, ab_ref, bb_ref), rows_b, True))
        ar = [o[0] for o in ops]
        bk = [o[1] for o in ops]
        v_h = [o[2] for o in ops]
        g_end = [o[3] for o in ops]
        s0 = [s_ref[i] for i in streams]
        g = [_dot_nt(ar[i], bk[i]) for i in streams]
        hs = [_dot_nt(ar[i], s0[i].astype(BF16)) for i in streams]
        g_top = [jnp.where(masks[i][0], g[i][:CHUNK], 0.0) for i in streams]
        g_bot = [jnp.where(masks[i][1], g[i][CHUNK:], 0.0).astype(BF16) for i in streams]
        vv = [jnp.concatenate([v_h[i], v_h[i]], axis=0) for i in streams]
        w_in = [hs[i][:CHUNK] + _dot(jnp.where(left, 0.0, g_top[i]).astype(BF16), vv[i]) for i in streams]
        z = [jnp.where(left, g_top[i], eye_right) for i in streams]
        for _ in range(CHUNK.bit_length() - 1):
            r = [_dot(z[i][:, :CHUNK].astype(BF16), z[i].astype(BF16)) for i in streams]
            z = [r[i] + jnp.where(left, 0.0, z[i]) for i in streams]
        t_inv = [jnp.where(left, 0.0, z[i]).astype(BF16) for i in streams]
        ww = [jnp.concatenate([w_in[i], w_in[i]], axis=0).astype(BF16) for i in streams]
        uv = [jnp.concatenate([_dot(t_inv[i], ww[i]).astype(BF16), v_h[i]], axis=0) for i in streams]
        y = [hs[i][CHUNK:] + _dot(g_bot[i], uv[i]) for i in streams]
        ds = [_dot_tn(uv[i], bk[i]) for i in streams]
        for i in streams:
            s_ref[i] = (s0[i] + ds[i]) * g_end[i]
            h = i % RW_HEADS
            sl = slice(h * RW_N, (h + 1) * RW_N)
            if i < RW_HEADS:
                yf_ref[rows_f, sl] = y[i]
            else:
                yb_ref[rows_b, sl] = y[i]
        return carry

    lax.fori_loop(0, nchunk, chunk, 0)


def rwkv_scan(r, v, a, lw0, k0, b0, lw1, k1, b1, bsz, seq):
    m = r.shape[0]
    bt = _pick(seq, 256)
    nb = seq // bt
    fwd = pl.BlockSpec((bt, WIDTH), lambda bi, i: (bi * nb + i, 0))
    bwd = pl.BlockSpec((bt, WIDTH), lambda bi, i: (bi * nb + nb - 1 - i, 0))
    return pl.pallas_call(
        functools.partial(_scan_kernel, nchunk=bt // CHUNK),
        grid=(bsz, nb),
        in_specs=[fwd] * 6 + [bwd] * 6,
        out_specs=[fwd, bwd],
        out_shape=[jax.ShapeDtypeStruct((m, WIDTH), F32)] * 2,
        scratch_shapes=[pltpu.VMEM((2 * RW_HEADS, RW_N, RW_N), F32)],
        compiler_params=_cparams(("parallel", "arbitrary")),
        name="rwkv_scan",
    )(r, lw0, k0, v, a, b0, r, lw1, k1, v, a, b1)


def _rwkv_post_kernel(y0_ref, y1_ref, bonus_ref, g_ref, lg_ref, lb_ref, ones_ref, o_ref):
    ones_bd = ones_ref[...]
    y = y0_ref[...] + y1_ref[...]
    mu = _seg_sum(y, ones_bd) * (1.0 / RW_N)
    yc = y - mu
    var = _seg_sum(yc * yc, ones_bd) * (1.0 / RW_N)
    yn = yc * lax.rsqrt(var + LNX_EPS) * lg_ref[...] + lb_ref[...]
    o_ref[...] = ((yn + bonus_ref[...]) * g_ref[...]).astype(o_ref.dtype)


def rwkv_post(y0, y1, bonus, g, lg, lb, ones_bd):
    m = y0.shape[0]
    bm = _pick(m, 512)
    spec = pl.BlockSpec((bm, WIDTH), lambda i: (i, 0))
    full = lambda i: (0, 0)
    return pl.pallas_call(
        _rwkv_post_kernel,
        grid=(m // bm,),
        in_specs=[spec] * 4 + [pl.BlockSpec((1, WIDTH), full), pl.BlockSpec((1, WIDTH), full),
                               pl.BlockSpec((SEG_LANES, SEG_LANES), full)],
        out_specs=spec,
        out_shape=jax.ShapeDtypeStruct((m, WIDTH), BF16),
        compiler_params=_cparams(("parallel",)),
        name="rwkv_post",
    )(y0, y1, bonus, g, lg, lb, ones_bd)


def _conv_kernel(u_ref, up_ref, un_ref, w_ref, o_ref):
    i = pl.program_id(1)
    u = u_ref[...].astype(F32)
    hp = u[:, WIDTH:2 * WIDTH] * u[:, 2 * WIDTH:]
    pr = up_ref[...].astype(F32)[CONV_HALO - 1:CONV_HALO, :]
    nr = un_ref[...].astype(F32)[0:1, :]
    prev_row = jnp.where(i == 0, 0.0, pr[:, WIDTH:2 * WIDTH] * pr[:, 2 * WIDTH:])
    next_row = jnp.where(i == pl.num_programs(1) - 1, 0.0, nr[:, WIDTH:2 * WIDTH] * nr[:, 2 * WIDTH:])
    hm, hn = _shifted(hp, prev_row, next_row)
    conv = w_ref[0:1, :] * hm + w_ref[1:2, :] * hp + w_ref[2:3, :] * hn
    o_ref[...] = (u[:, :WIDTH] * conv).astype(o_ref.dtype)


def conv_branch(u_c, conv_w, bsz, seq):
    m = u_c.shape[0]
    bm = _pick(seq, 512)
    nb = seq // bm
    cur, prev, nxt = _halo_specs(bm, CONV_COLS, nb, bsz, CONV_HALO)
    return pl.pallas_call(
        _conv_kernel,
        grid=(bsz, nb),
        in_specs=[cur, prev, nxt, pl.BlockSpec((3, WIDTH), lambda b, i: (0, 0))],
        out_specs=pl.BlockSpec((bm, WIDTH), lambda b, i: (b * nb + i, 0)),
        out_shape=jax.ShapeDtypeStruct((m, WIDTH), BF16),
        compiler_params=_cparams(("parallel", "arbitrary")),
        name="conv_branch",
    )(u_c, u_c, u_c, conv_w)


def _merge_kernel(ya_ref, yb_ref, yc_ref, g0_ref, g1_ref, g2_ref, w_ref, o_ref):
    acc = g0_ref[...].astype(F32) * _dot(ya_ref[...], w_ref[0])
    acc = acc + g1_ref[...].astype(F32) * _dot(yb_ref[...], w_ref[1])
    acc = acc + g2_ref[...].astype(F32) * _dot(yc_ref[...], w_ref[2])
    o_ref[...] = acc.astype(o_ref.dtype)


def merge_branches(ya, yb, yc, gates, w_branch):
    m = ya.shape[0]
    bm = _pick(m, 1024)
    bn = 1024
    nj = D_MODEL // bn
    yspec = pl.BlockSpec((bm, WIDTH), lambda j, i: (i, 0))
    gspecs = [pl.BlockSpec((bm, bn), functools.partial(lambda j, i, t: (i, t * nj + j), t=t))
              for t in range(3)]
    return pl.pallas_call(
        _merge_kernel,
        grid=(nj, m // bm),
        in_specs=[yspec] * 3 + gspecs + [pl.BlockSpec((3, WIDTH, bn), lambda j, i: (0, 0, j))],
        out_specs=pl.BlockSpec((bm, bn), lambda j, i: (i, j)),
        out_shape=jax.ShapeDtypeStruct((m, D_MODEL), BF16),
        compiler_params=_cparams(("arbitrary", "arbitrary")),
        name="merge",
    )(ya, yb, yc, gates, gates, gates, w_branch)


def _out_proj_ln_kernel(x_ref, w_ref, res_ref, g_ref, b_ref, o_ref, ob_ref):
    y = _ln_math(ALPHA * res_ref[...] + _dot(x_ref[...], w_ref[...]), g_ref[...], b_ref[...])
    o_ref[...] = y
    ob_ref[...] = y.astype(BF16)


def out_proj_ln(x, w, res, g, b):
    m, k = x.shape
    n = w.shape[1]
    bm = _pick(m, 512)
    row = pl.BlockSpec((bm, n), lambda i: (i, 0))
    vec = pl.BlockSpec((1, n), lambda i: (0, 0))
    return pl.pallas_call(
        _out_proj_ln_kernel,
        grid=(m // bm,),
        in_specs=[pl.BlockSpec((bm, k), lambda i: (i, 0)), pl.BlockSpec((k, n), lambda i: (0, 0)),
                  row, vec, vec],
        out_specs=[row, row],
        out_shape=[jax.ShapeDtypeStruct((m, n), F32), jax.ShapeDtypeStruct((m, n), BF16)],
        compiler_params=_cparams(("parallel",)),
        name="out_proj_ln",
    )(x, w, res, g, b)


def _ffn_kernel(x_ref, wg_ref, wu_ref, wo_ref, res_ref, g_ref, b_ref, o_ref, ob_ref, acc_ref):
    j = pl.program_id(1)

    @pl.when(j == 0)
    def _():
        acc_ref[...] = jnp.zeros(acc_ref.shape, F32)

    x = x_ref[...]
    gate = _dot(x, wg_ref[...])
    up = _dot(x, wu_ref[...])
    f = (gate * jax.nn.sigmoid(gate) * up).astype(BF16)
    acc_ref[...] += _dot(f, wo_ref[...])

    @pl.when(j == pl.num_programs(1) - 1)
    def _():
        y = _ln_math(ALPHA * res_ref[...] + acc_ref[...], g_ref[...], b_ref[...])
        o_ref[...] = y
        ob_ref[...] = y.astype(BF16)


def ffn(x, w_in, w_out, res, g, b):
    m, k = x.shape
    n = w_out.shape[1]
    bm = _pick(m, 512)
    bn = 512
    nj = D_FF // bn
    row = pl.BlockSpec((bm, n), lambda i, j: (i, 0))
    vec = pl.BlockSpec((1, n), lambda i, j: (0, 0))
    return pl.pallas_call(
        _ffn_kernel,
        grid=(m // bm, nj),
        in_specs=[pl.BlockSpec((bm, k), lambda i, j: (i, 0)),
                  pl.BlockSpec((k, bn), lambda i, j: (0, j)),
                  pl.BlockSpec((k, bn), lambda i, j: (0, nj + j)),
                  pl.BlockSpec((bn, n), lambda i, j: (j, 0)),
                  row, vec, vec],
        out_specs=[row, row],
        out_shape=[jax.ShapeDtypeStruct((m, n), F32), jax.ShapeDtypeStruct((m, n), BF16)],
        scratch_shapes=[pltpu.VMEM((bm, n), F32)],
        compiler_params=_cparams(("parallel", "arbitrary")),
        name="ffn",
    )(x, w_in, w_in, w_out, res, g, b)


def _prep_params(w_in, q_norm_g, w_uq, kv_norm_g, w_ukv, rwkv_mu, rwkv_w0, rwkv_w_up, rwkv_a0,
                 rwkv_a_up, rwkv_g_up, rwkv_k_k, rwkv_k_a, rwkv_r_k, rwkv_lnx_g, rwkv_lnx_b,
                 conv_w, w_branch, w_out, ln1_g, ln1_b, w_ffn_in, w_ffn_out, ln2_g, ln2_b):
    nl = w_in.shape[0]
    z = lambda *s: jnp.zeros(s, F32)
    o = MLA_COLS
    kr = w_in[:, :, o - ROPE:o]
    kr_rot = jnp.concatenate([-kr[..., ROPE // 2:], kr[..., :ROPE // 2]], axis=-1)
    w_mla = jnp.concatenate([w_in[:, :, :o], z(nl, D_MODEL, 64), kr_rot, z(nl, D_MODEL, 64)], axis=-1)
    w_rw = jnp.concatenate([w_in[:, :, o:o + RWKV_COLS], z(nl, D_MODEL, RWKV_PAD - RWKV_COLS)], axis=-1)
    o += RWKV_COLS
    w_cv = w_in[:, :, o:o + CONV_COLS]
    o += CONV_COLS
    w_gt = w_in[:, :, o:]

    wq = w_uq.reshape(nl, Q_LORA, MLA_HEADS, NOPE + ROPE)
    x1 = wq[..., NOPE:NOPE + ROPE // 2]
    x2 = wq[..., NOPE + ROPE // 2:]
    pad = z(nl, Q_LORA, MLA_HEADS, 128 - ROPE)
    wq_all = jnp.concatenate([wq[..., :NOPE].reshape(nl, Q_LORA, WIDTH),
                              jnp.concatenate([x1, x2, pad], -1).reshape(nl, Q_LORA, WIDTH),
                              jnp.concatenate([-x2, x1, pad], -1).reshape(nl, Q_LORA, WIDTH)], axis=-1)
    wkv = w_ukv.reshape(nl, KV_LORA, MLA_HEADS, NOPE + VDIM)
    wkv_all = jnp.concatenate([wkv[..., :NOPE].reshape(nl, KV_LORA, WIDTH),
                               wkv[..., NOPE:].reshape(nl, KV_LORA, WIDTH)], axis=-1)

    zl = z(nl, DECAY_LORA, WIDTH)
    wup = jnp.concatenate([jnp.concatenate([rwkv_w_up[:, 0], zl], -1),
                           jnp.concatenate([zl, rwkv_w_up[:, 1]], -1)], axis=1)
    aup = jnp.concatenate([jnp.concatenate([rwkv_a_up[:, 0], zl], -1),
                           jnp.concatenate([zl, rwkv_a_up[:, 1]], -1)], axis=1)
    gup = jnp.concatenate([rwkv_g_up, z(nl, GD_PAD - GATE_LORA, WIDTH)], axis=1)
    mu = jnp.concatenate([rwkv_mu, z(nl, 2, RWKV_PAD - RWKV_COLS)], axis=-1)
    head = jnp.arange(SEG_LANES) // RW_N
    ones_bd = (head[:, None] == head[None, :]).astype(BF16)
    row = lambda t: t.reshape(nl, 1, -1)
    return dict(
        w_mla=w_mla.astype(BF16), w_rw=w_rw.astype(BF16), w_cv=w_cv.astype(BF16), w_gt=w_gt.astype(BF16),
        qg=row(q_norm_g), kvg=row(kv_norm_g), wq=wq_all.astype(BF16), wkv=wkv_all.astype(BF16),
        mu=mu, w0=row(rwkv_w0), a0=row(rwkv_a0), wup=wup.astype(BF16), aup=aup.astype(BF16),
        gup=gup.astype(BF16), k_k=row(rwkv_k_k), k_a=row(rwkv_k_a), r_k=row(rwkv_r_k),
        lnx_g=row(rwkv_lnx_g), lnx_b=row(rwkv_lnx_b), conv_w=conv_w,
        w_branch=w_branch.astype(BF16), w_out=w_out.astype(BF16),
        ln1_g=row(ln1_g), ln1_b=row(ln1_b), w_ffn_in=w_ffn_in.astype(BF16),
        w_ffn_out=w_ffn_out.astype(BF16), ln2_g=row(ln2_g), ln2_b=row(ln2_b),
    ), ones_bd


def _rope_tables(seq):
    pos = jnp.arange(seq, dtype=F32)
    inv = ROPE_THETA ** (-jnp.arange(0, ROPE, 2, dtype=F32) / ROPE)
    ang = pos[:, None] * inv[None, :]
    c, s = jnp.cos(ang), jnp.sin(ang)
    zpad = jnp.zeros((seq, 128 - ROPE), F32)
    return jnp.concatenate([c, c, zpad], -1), jnp.concatenate([s, s, zpad], -1)


def _layer(x, xb, p, ones_bd, cosz, sinz, bsz, seq):
    u_a = matmul(xb, p["w_mla"], MLA_PAD, name="in_mla")
    u_b = matmul(xb, p["w_rw"], RWKV_PAD // 2, name="in_rwkv")
    u_c = matmul(xb, p["w_cv"], 1024, out_dtype=BF16, name="in_conv")
    gates = matmul(xb, p["w_gt"], 1024, out_dtype=BF16, name="in_gate", sigmoid=True)

    q, k, v = mla_prep(u_a, p["qg"], p["kvg"], p["wq"], p["wkv"], cosz, sinz, seq)
    y_a = attention(q, k, v, bsz, seq)

    pp = dict(p, ones_bd=ones_bd)
    r, vv, a, lw0, lw1, k0, k1, b0, b1, bonus, g = rwkv_prep(u_b, pp, bsz, seq)
    y0, y1 = rwkv_scan(r, vv, a, lw0, k0, b0, lw1, k1, b1, bsz, seq)
    y_b = rwkv_post(y0, y1, bonus, g, p["lnx_g"], p["lnx_b"], ones_bd)

    y_c = conv_branch(u_c, p["conv_w"], bsz, seq)

    merged = merge_branches(y_a, y_b, y_c, gates, p["w_branch"])
    h, hb = out_proj_ln(merged, p["w_out"], x, p["ln1_g"], p["ln1_b"])
    return ffn(hb, p["w_ffn_in"], p["w_ffn_out"], h, p["ln2_g"], p["ln2_b"])


def _trunk(x3, ln_g, ln_b, params, ones_bd):
    bsz, seq, d = x3.shape
    cosz, sinz = _rope_tables(seq)
    x, xb = layer_norm_in(x3.reshape(bsz * seq, d), ln_g.reshape(1, d), ln_b.reshape(1, d))
    for l in range(DEPTH):
        p = {name: t[l] for name, t in params.items()}
        x, xb = _layer(x, xb, p, ones_bd, cosz, sinz, bsz, seq)
    return x.reshape(bsz, seq, d)


def kernel(x_prompt, x_sample, ln_in_g, ln_in_b, w_in, q_norm_g, w_uq, kv_norm_g, w_ukv, rwkv_mu, rwkv_w0, rwkv_w_up, rwkv_a0, rwkv_a_up, rwkv_g_up, rwkv_k_k, rwkv_k_a, rwkv_r_k, rwkv_lnx_g, rwkv_lnx_b, conv_w, w_branch, w_out, ln1_g, ln1_b, w_ffn_in, w_ffn_out, ln2_g, ln2_b):
    params, ones_bd = _prep_params(w_in, q_norm_g, w_uq, kv_norm_g, w_ukv, rwkv_mu, rwkv_w0, rwkv_w_up,
                                   rwkv_a0, rwkv_a_up, rwkv_g_up, rwkv_k_k, rwkv_k_a, rwkv_r_k,
                                   rwkv_lnx_g, rwkv_lnx_b, conv_w, w_branch, w_out, ln1_g, ln1_b,
                                   w_ffn_in, w_ffn_out, ln2_g, ln2_b)
    y_prompt = _trunk(x_prompt, ln_in_g, ln_in_b, params, ones_bd)
    y_sample = _trunk(x_sample, ln_in_g, ln_in_b, params, ones_bd)
    return (y_prompt, y_sample)
```

```python
import functools

import jax
import jax.numpy as jnp
from jax import lax
from jax.experimental import pallas as pl
from jax.experimental.pallas import tpu as pltpu

F32 = jnp.float32
BF16 = jnp.bfloat16

D_MODEL = 2048
DEPTH = 4
WIDTH = D_MODEL // 2
MLA_HEADS = 8
NOPE = 128
ROPE = 64
VDIM = 128
Q_LORA = 768
KV_LORA = 512
ROPE_THETA = 10000.0
RW_HEADS = 16
RW_N = 64
DECAY_LORA = 64
AAA_LORA = 64
GATE_LORA = 160
D_FF = 5632
LN_EPS = 1e-5
RMS_EPS = 1e-6
LNX_EPS = 1e-5 * RW_N
ALPHA = (2 * DEPTH) ** 0.25
LOG2_E = 1.4426950408889634
EXP_NEG_HALF = 0.6065306597126334
MLA_COLS = Q_LORA + KV_LORA + ROPE
RWKV_COLS = 3 * WIDTH + 2 * DECAY_LORA + 2 * AAA_LORA + GATE_LORA
CONV_COLS = 3 * WIDTH
GATE_COLS = 3 * D_MODEL
MLA_PAD = 1536
RWKV_PAD = 3584
GD_PAD = 256
QK_PAD = 256
CHUNK = 64
CONV_HALO = 16
SEG_LANES = 256

VMEM_LIMIT = 56 * 1024 * 1024


def _cparams(sem):
    return pltpu.CompilerParams(dimension_semantics=sem, vmem_limit_bytes=VMEM_LIMIT)


def _dot(a, b):
    return jnp.dot(a, b, preferred_element_type=F32)


def _dot_nt(a, b):
    return lax.dot_general(a, b, (((1,), (1,)), ((), ())), preferred_element_type=F32)


def _dot_tn(a, b):
    return lax.dot_general(a, b, (((0,), (0,)), ((), ())), preferred_element_type=F32)


def _split2(x):
    hi = x.astype(BF16)
    lo = (x - hi.astype(F32)).astype(BF16)
    return hi, lo


def _split3(x):
    hi = x.astype(BF16)
    r1 = x - hi.astype(F32)
    mid = r1.astype(BF16)
    lo = (r1 - mid.astype(F32)).astype(BF16)
    return hi, mid, lo


def _seg_sum(x, ones_bd):
    hi, lo = _split2(x)
    parts = []
    for c in range(0, x.shape[1], SEG_LANES):
        parts.append(_dot(hi[:, c:c + SEG_LANES], ones_bd) + _dot(lo[:, c:c + SEG_LANES], ones_bd))
    return jnp.concatenate(parts, axis=1)


def _pick(n, target):
    b = min(n, target)
    while n % b:
        b //= 2
    return b


def _ln_math(x, g, b):
    mu = jnp.mean(x, axis=-1, keepdims=True)
    xc = x - mu
    var = jnp.mean(xc * xc, axis=-1, keepdims=True)
    return xc * lax.rsqrt(var + LN_EPS) * g + b


def _ln_kernel(x_ref, g_ref, b_ref, o_ref, ob_ref):
    y = _ln_math(x_ref[...], g_ref[...], b_ref[...])
    o_ref[...] = y
    ob_ref[...] = y.astype(BF16)


def layer_norm_in(x, g, b):
    m, d = x.shape
    bm = _pick(m, 512)
    return pl.pallas_call(
        _ln_kernel,
        grid=(m // bm,),
        in_specs=[pl.BlockSpec((bm, d), lambda i: (i, 0)),
                  pl.BlockSpec((1, d), lambda i: (0, 0)),
                  pl.BlockSpec((1, d), lambda i: (0, 0))],
        out_specs=[pl.BlockSpec((bm, d), lambda i: (i, 0)),
                   pl.BlockSpec((bm, d), lambda i: (i, 0))],
        out_shape=[jax.ShapeDtypeStruct((m, d), F32), jax.ShapeDtypeStruct((m, d), BF16)],
        compiler_params=_cparams(("parallel",)),
        name="ln_in",
    )(x, g, b)


def _mm_kernel(x_ref, w_ref, o_ref, *, sigmoid):
    x = x_ref[...]
    if sigmoid:
        half = w_ref.shape[1] // 2
        for c in (0, half):
            o_ref[:, c:c + half] = jax.nn.sigmoid(_dot(x, w_ref[:, c:c + half])).astype(o_ref.dtype)
    else:
        o_ref[...] = _dot(x, w_ref[...]).astype(o_ref.dtype)


def matmul(x, w, bn, out_dtype=F32, name="mm", sigmoid=False):
    m, k = x.shape
    n = w.shape[1]
    bm = _pick(m, 1024)
    return pl.pallas_call(
        functools.partial(_mm_kernel, sigmoid=sigmoid),
        grid=(m // bm, n // bn),
        in_specs=[pl.BlockSpec((bm, k), lambda i, j: (i, 0)),
                  pl.BlockSpec((k, bn), lambda i, j: (0, j))],
        out_specs=pl.BlockSpec((bm, bn), lambda i, j: (i, j)),
        out_shape=jax.ShapeDtypeStruct((m, n), out_dtype),
        compiler_params=_cparams(("parallel", "arbitrary")),
        name=name,
    )(x, w)


def _rms(x, g):
    return x * lax.rsqrt(jnp.mean(x * x, axis=-1, keepdims=True) + RMS_EPS) * g


def _mla_prep_kernel(u_ref, qg_ref, kvg_ref, wq_ref, wkv_ref, cos_ref, sin_ref,
                     q_ref, k_ref, v_ref):
    scale = (NOPE + ROPE) ** -0.5 * LOG2_E
    cz = cos_ref[...]
    sz = sin_ref[...]
    cq = _rms(u_ref[:, :Q_LORA], qg_ref[...]).astype(BF16)
    pq = _dot(cq, wq_ref[...])
    ckv = _rms(u_ref[:, Q_LORA:Q_LORA + KV_LORA], kvg_ref[...]).astype(BF16)
    pkv = _dot(ckv, wkv_ref[...])
    o = Q_LORA + KV_LORA
    krz = (u_ref[:, o:o + 128] * cz + u_ref[:, o + 128:o + 256] * sz).astype(BF16)
    lane = lax.broadcasted_iota(jnp.int32, (cz.shape[0], 128), 1)
    one_col = jnp.where(lane == 0, 1.0, 0.0).astype(BF16)
    for h in range(MLA_HEADS):
        a, b = h * 128, (h + 1) * 128
        q_ref[:, h * QK_PAD:h * QK_PAD + 128] = (pq[:, a:b] * scale).astype(BF16)
        q_ref[:, h * QK_PAD + 128:(h + 1) * QK_PAD] = (
            (pq[:, WIDTH + a:WIDTH + b] * cz + pq[:, 2 * WIDTH + a:2 * WIDTH + b] * sz) * scale
        ).astype(BF16)
        k_ref[:, h * QK_PAD:h * QK_PAD + 128] = pkv[:, a:b].astype(BF16)
        k_ref[:, h * QK_PAD + 128:(h + 1) * QK_PAD] = krz
        v_ref[:, h * QK_PAD:h * QK_PAD + 128] = pkv[:, WIDTH + a:WIDTH + b].astype(BF16)
        v_ref[:, h * QK_PAD + 128:(h + 1) * QK_PAD] = one_col


def mla_prep(u_a, qg, kvg, wq, wkv, cosz, sinz, seq):
    m = u_a.shape[0]
    bm = _pick(seq, 512)
    nb = seq // bm
    full = lambda i: (0, 0)
    return pl.pallas_call(
        _mla_prep_kernel,
        grid=(m // bm,),
        in_specs=[pl.BlockSpec((bm, MLA_PAD), lambda i: (i, 0)),
                  pl.BlockSpec((1, Q_LORA), full),
                  pl.BlockSpec((1, KV_LORA), full),
                  pl.BlockSpec(wq.shape, full),
                  pl.BlockSpec(wkv.shape, full),
                  pl.BlockSpec((bm, 128), lambda i: (i % nb, 0)),
                  pl.BlockSpec((bm, 128), lambda i: (i % nb, 0))],
        out_specs=[pl.BlockSpec((bm, MLA_HEADS * QK_PAD), lambda i: (i, 0)),
                   pl.BlockSpec((bm, MLA_HEADS * QK_PAD), lambda i: (i, 0)),
                   pl.BlockSpec((bm, MLA_HEADS * QK_PAD), lambda i: (i, 0))],
        out_shape=[jax.ShapeDtypeStruct((m, MLA_HEADS * QK_PAD), BF16)] * 3,
        compiler_params=_cparams(("parallel",)),
        name="mla_prep",
    )(u_a, qg, kvg, wq, wkv, cosz, sinz)


def _attn_kernel(q_ref, k_ref, v_ref, o_ref, *, nkv, bk):
    q = q_ref[...]
    m = jnp.full((q.shape[0], 1), -jnp.inf, F32)
    acc = jnp.zeros((q.shape[0], QK_PAD), F32)
    s = _dot_nt(q, k_ref[0:bk, :])
    for j in range(nkv):
        if j + 1 < nkv:
            s_next = _dot_nt(q, k_ref[(j + 1) * bk:(j + 2) * bk, :])
        m_new = jnp.maximum(m, jnp.max(s, axis=-1, keepdims=True))
        alpha = jnp.exp2(m - m_new)
        p = jnp.exp2((s - m_new).astype(BF16))
        acc = alpha * acc + _dot(p, v_ref[j * bk:(j + 1) * bk, :])
        m = m_new
        if j + 1 < nkv:
            s = s_next
    o_ref[...] = (acc[:, :VDIM] / acc[:, VDIM:VDIM + 1]).astype(o_ref.dtype)


def attention(q, k, v, bsz, seq):
    m = q.shape[0]
    bq = _pick(seq, 1024)
    bk = _pick(seq, 1024)
    nq = seq // bq
    return pl.pallas_call(
        functools.partial(_attn_kernel, nkv=seq // bk, bk=bk),
        grid=(bsz, MLA_HEADS, nq),
        in_specs=[pl.BlockSpec((bq, QK_PAD), lambda b, h, i: (b * nq + i, h)),
                  pl.BlockSpec((seq, QK_PAD), lambda b, h, i: (b, h)),
                  pl.BlockSpec((seq, QK_PAD), lambda b, h, i: (b, h))],
        out_specs=pl.BlockSpec((bq, VDIM), lambda b, h, i: (b * nq + i, h)),
        out_shape=jax.ShapeDtypeStruct((m, WIDTH), BF16),
        compiler_params=_cparams(("parallel", "parallel", "arbitrary")),
        name="mla_attn",
    )(q, k, v)


def _shifted(u, prev_row, next_row):
    n = u.shape[0]
    row = lax.broadcasted_iota(jnp.int32, (n, 1), 0)
    up = jnp.where(row == 0, prev_row, pltpu.roll(u, 1, axis=0))
    un = jnp.where(row == n - 1, next_row, pltpu.roll(u, n - 1, axis=0))
    return up, un


def _halo_specs(bm, width, nb, bsz, hr):
    per = bm // hr
    last = bsz * nb * per - 1
    cur = pl.BlockSpec((bm, width), lambda b, i: (b * nb + i, 0))
    prev = pl.BlockSpec((hr, width), lambda b, i: (jnp.maximum((b * nb + i) * per - 1, 0), 0))
    nxt = pl.BlockSpec((hr, width), lambda b, i: (jnp.minimum((b * nb + i + 1) * per, last), 0))
    return cur, prev, nxt


def _rwkv_prep_kernel(u_ref, up_ref, un_ref, mu_ref, w0_ref, a0_ref, wup_ref, aup_ref, gup_ref,
                      kk_ref, ka_ref, rk_ref, ones_ref,
                      r_o, v_o, a_o, lw0_o, lw1_o, k0_o, k1_o, b0_o, b1_o, bonus_o, g_o):
    i = pl.program_id(1)
    u = u_ref[...]
    prev_row = jnp.where(i == 0, 0.0, up_ref[7:8, :])
    next_row = jnp.where(i == pl.num_programs(1) - 1, 0.0, un_ref[0:1, :])
    up, un = _shifted(u, prev_row, next_row)
    us = u + mu_ref[0:1, :] * (up - u) + mu_ref[1:2, :] * (un - u)
    r = us[:, 0:WIDTH]
    k = us[:, WIDTH:2 * WIDTH]
    v = us[:, 2 * WIDTH:3 * WIDTH]
    o = 3 * WIDTH
    wd = us[:, o:o + 2 * DECAY_LORA]
    ad = us[:, o + 128:o + 128 + 2 * AAA_LORA]
    gd = us[:, o + 256:o + 256 + GD_PAD]
    g_o[...] = _dot(jax.nn.sigmoid(gd).astype(BF16), gup_ref[...]).astype(g_o.dtype)
    t = w0_ref[...] + _dot(jnp.tanh(wd).astype(BF16), wup_ref[...])
    lw = -EXP_NEG_HALF * jax.nn.sigmoid(t)
    al = jax.nn.sigmoid(a0_ref[...] + _dot(ad.astype(BF16), aup_ref[...]))
    ones_bd = ones_ref[...]
    kkr = k * kk_ref[...]
    kk = kkr * lax.rsqrt(jnp.maximum(_seg_sum(kkr * kkr, ones_bd), 1e-24))
    ka = ka_ref[...]
    a_0 = al[:, :WIDTH]
    a_1 = al[:, WIDTH:]
    k_0 = k * (1.0 + (a_0 - 1.0) * ka)
    k_1 = k * (1.0 + (a_1 - 1.0) * ka)
    r_o[...] = r.astype(r_o.dtype)
    v_o[...] = v.astype(v_o.dtype)
    a_o[...] = (-kk).astype(a_o.dtype)
    lw0_o[...] = lw[:, :WIDTH]
    lw1_o[...] = lw[:, WIDTH:]
    k0_o[...] = k_0.astype(k0_o.dtype)
    k1_o[...] = k_1.astype(k1_o.dtype)
    b0_o[...] = (kk * a_0).astype(b0_o.dtype)
    b1_o[...] = (kk * a_1).astype(b1_o.dtype)
    bonus_o[...] = (_seg_sum(r * rk_ref[...] * (k_0 + k_1), ones_bd) * v).astype(bonus_o.dtype)


def rwkv_prep(u_b, p, bsz, seq):
    m = u_b.shape[0]
    bm = _pick(seq, 128)
    nb = seq // bm
    cur, prev, nxt = _halo_specs(bm, RWKV_PAD, nb, bsz, 8)
    full = lambda b, i: (0, 0)
    consts = [p["mu"], p["w0"], p["a0"], p["wup"], p["aup"], p["gup"], p["k_k"], p["k_a"], p["r_k"],
              p["ones_bd"]]
    out_spec = pl.BlockSpec((bm, WIDTH), lambda b, i: (b * nb + i, 0))
    return pl.pallas_call(
        _rwkv_prep_kernel,
        grid=(bsz, nb),
        in_specs=[cur, prev, nxt] + [pl.BlockSpec(c.shape, full) for c in consts],
        out_specs=[out_spec] * 11,
        out_shape=[jax.ShapeDtypeStruct((m, WIDTH), dt)
                   for dt in (BF16, BF16, BF16, F32, F32, BF16, BF16, BF16, BF16, BF16, BF16)],
        compiler_params=_cparams(("parallel", "arbitrary")),
        name="rwkv_prep",
    )(u_b, u_b, u_b, *consts)


def _scan_chunk_operands(refs, rows, reverse):
    r_ref, lw_ref, k_ref, v_ref, a_ref, b_ref = refs
    ti = lax.broadcasted_iota(jnp.int32, (CHUNK, CHUNK), 0)
    si = lax.broadcasted_iota(jnp.int32, (CHUNK, CHUNK), 1)
    tri = ((si >= ti) if reverse else (si <= ti)).astype(BF16)
    last = 0 if reverse else CHUNK - 1
    lw = lw_ref[rows, :]
    l_hi, l_mid, l_lo = _split3(lw)
    cum = _dot(tri, l_hi) + _dot(tri, l_mid) + _dot(tri, l_lo)
    gam = jnp.exp(cum)
    ginv = jnp.exp(-cum)
    rt = r_ref[rows, :].astype(F32) * gam
    at = a_ref[rows, :].astype(F32) * jnp.exp(cum - lw)
    kt = k_ref[rows, :].astype(F32) * ginv
    bt = b_ref[rows, :].astype(F32) * ginv
    v = v_ref[rows, :].astype(F32)
    g_end = gam[last:last + 1, :]
    out = []
    for h in range(RW_HEADS):
        sl = slice(h * RW_N, (h + 1) * RW_N)
        ar = jnp.concatenate([at[:, sl].astype(BF16), rt[:, sl].astype(BF16)], axis=0)
        bk = jnp.concatenate([bt[:, sl].astype(BF16), kt[:, sl].astype(BF16)], axis=0)
        out.append((ar, bk, v[:, sl].astype(BF16), g_end[:, sl]))
    return out


def _scan_kernel(rf_ref, lwf_ref, kf_ref, vf_ref, af_ref, bf_ref,
                 rb_ref, lwb_ref, kb_ref, vb_ref, ab_ref, bb_ref,
                 yf_ref, yb_ref, s_ref, *, nchunk):
    @pl.when(pl.program_id(1) == 0)
    def _():
        s_ref[...] = jnp.zeros(s_ref.shape, F32)

    ti = lax.broadcasted_iota(jnp.int32, (CHUNK, 2 * CHUNK), 0)
    lane = lax.broadcasted_iota(jnp.int32, (CHUNK, 2 * CHUNK), 1)
    si = lane & (CHUNK - 1)
    left = lane < CHUNK
    eye_right = jnp.where(lane == ti + CHUNK, 1.0, 0.0)
    masks = []
    for reverse in (False, True):
        strict = (si > ti) if reverse else (si < ti)
        incl = (si >= ti) if reverse else (si <= ti)
        masks += [(strict, incl)] * RW_HEADS
    n = 2 * RW_HEADS
    streams = range(n)

    def chunk(ci, carry):
        rows_f = pl.ds(pl.multiple_of(ci * CHUNK, CHUNK), CHUNK)
        rows_b = pl.ds(pl.multiple_of((nchunk - 1 - ci) * CHUNK, CHUNK), CHUNK)
        ops = (_scan_chunk_operands((rf_ref, lwf_ref, kf_ref, vf_ref, af_ref, bf_ref), rows_f, False)
               + _scan_chunk_operands((rb_ref, lwb_ref, kb_ref, vb_ref, ab_ref, bb_ref), rows_b, True))
        ar = [o[0] for o in ops]
        bk = [o[1] for o in ops]
        v_h = [o[2] for o in ops]
        g_end = [o[3] for o in ops]
        s0 = [s_ref[i] for i in streams]
        g = [_dot_nt(ar[i], bk[i]) for i in streams]
        hs = [_dot_nt(ar[i], s0[i].astype(BF16)) for i in streams]
        g_top = [jnp.where(masks[i][0], g[i][:CHUNK], 0.0) for i in streams]
        g_bot = [jnp.where(masks[i][1], g[i][CHUNK:], 0.0).astype(BF16) for i in streams]
        vv = [jnp.concatenate([v_h[i], v_h[i]], axis=0) for i in streams]
        w_in = [hs[i][:CHUNK] + _dot(jnp.where(left, 0.0, g_top[i]).astype(BF16), vv[i]) for i in streams]
        z = [jnp.where(left, g_top[i], eye_right) for i in streams]
        for _ in range(CHUNK.bit_length() - 1):
            r = [_dot(z[i][:, :CHUNK].astype(BF16), z[i].astype(BF16)) for i in streams]
            z = [r[i] + jnp.where(left, 0.0, z[i]) for i in streams]
        t_inv = [jnp.where(left, 0.0, z[i]).astype(BF16) for i in streams]
        ww = [jnp.concatenate([w_in[i], w_in[i]], axis=0).astype(BF16) for i in streams]
        uv = [jnp.concatenate([_dot(t_inv[i], ww[i]).astype(BF16), v_h[i]], axis=0) for i in streams]
        y = [hs[i][CHUNK:] + _dot(g_bot[i], uv[i]) for i in streams]
        ds = [_dot_tn(uv[i], bk[i]) for i in streams]
        for i in streams:
            s_ref[i] = (s0[i] + ds[i]) * g_end[i]
            h = i % RW_HEADS
            sl = slice(h * RW_N, (h + 1) * RW_N)
            if i < RW_HEADS:
                yf_ref[rows_f, sl] = y[i]
            else:
                yb_ref[rows_b, sl] = y[i]
        return carry

    lax.fori_loop(0, nchunk, chunk, 0)


def rwkv_scan(r, v, a, lw0, k0, b0, lw1, k1, b1, bsz, seq):
    m = r.shape[0]
    bt = _pick(seq, 256)
    nb = seq // bt
    fwd = pl.BlockSpec((bt, WIDTH), lambda bi, i: (bi * nb + i, 0))
    bwd = pl.BlockSpec((bt, WIDTH), lambda bi, i: (bi * nb + nb - 1 - i, 0))
    return pl.pallas_call(
        functools.partial(_scan_kernel, nchunk=bt // CHUNK),
        grid=(bsz, nb),
        in_specs=[fwd] * 6 + [bwd] * 6,
        out_specs=[fwd, bwd],
        out_shape=[jax.ShapeDtypeStruct((m, WIDTH), F32)] * 2,
        scratch_shapes=[pltpu.VMEM((2 * RW_HEADS, RW_N, RW_N), F32)],
        compiler_params=_cparams(("parallel", "arbitrary")),
        name="rwkv_scan",
    )(r, lw0, k0, v, a, b0, r, lw1, k1, v, a, b1)


def _rwkv_post_kernel(y0_ref, y1_ref, bonus_ref, g_ref, lg_ref, lb_ref, ones_ref, o_ref):
    ones_bd = ones_ref[...]
    y = y0_ref[...] + y1_ref[...]
    mu = _seg_sum(y, ones_bd) * (1.0 / RW_N)
    yc = y - mu
    var = _seg_sum(yc * yc, ones_bd) * (1.0 / RW_N)
    yn = yc * lax.rsqrt(var + LNX_EPS) * lg_ref[...] + lb_ref[...]
    o_ref[...] = ((yn + bonus_ref[...]) * g_ref[...]).astype(o_ref.dtype)


def rwkv_post(y0, y1, bonus, g, lg, lb, ones_bd):
    m = y0.shape[0]
    bm = _pick(m, 512)
    spec = pl.BlockSpec((bm, WIDTH), lambda i: (i, 0))
    full = lambda i: (0, 0)
    return pl.pallas_call(
        _rwkv_post_kernel,
        grid=(m // bm,),
        in_specs=[spec] * 4 + [pl.BlockSpec((1, WIDTH), full), pl.BlockSpec((1, WIDTH), full),
                               pl.BlockSpec((SEG_LANES, SEG_LANES), full)],
        out_specs=spec,
        out_shape=jax.ShapeDtypeStruct((m, WIDTH), BF16),
        compiler_params=_cparams(("parallel",)),
        name="rwkv_post",
    )(y0, y1, bonus, g, lg, lb, ones_bd)


def _conv_kernel(u_ref, up_ref, un_ref, w_ref, o_ref):
    i = pl.program_id(1)
    u = u_ref[...].astype(F32)
    hp = u[:, WIDTH:2 * WIDTH] * u[:, 2 * WIDTH:]
    pr = up_ref[...].astype(F32)[CONV_HALO - 1:CONV_HALO, :]
    nr = un_ref[...].astype(F32)[0:1, :]
    prev_row = jnp.where(i == 0, 0.0, pr[:, WIDTH:2 * WIDTH] * pr[:, 2 * WIDTH:])
    next_row = jnp.where(i == pl.num_programs(1) - 1, 0.0, nr[:, WIDTH:2 * WIDTH] * nr[:, 2 * WIDTH:])
    hm, hn = _shifted(hp, prev_row, next_row)
    conv = w_ref[0:1, :] * hm + w_ref[1:2, :] * hp + w_ref[2:3, :] * hn
    o_ref[...] = (u[:, :WIDTH] * conv).astype(o_ref.dtype)


def conv_branch(u_c, conv_w, bsz, seq):
    m = u_c.shape[0]
    bm = _pick(seq, 512)
    nb = seq // bm
    cur, prev, nxt = _halo_specs(bm, CONV_COLS, nb, bsz, CONV_HALO)
    return pl.pallas_call(
        _conv_kernel,
        grid=(bsz, nb),
        in_specs=[cur, prev, nxt, pl.BlockSpec((3, WIDTH), lambda b, i: (0, 0))],
        out_specs=pl.BlockSpec((bm, WIDTH), lambda b, i: (b * nb + i, 0)),
        out_shape=jax.ShapeDtypeStruct((m, WIDTH), BF16),
        compiler_params=_cparams(("parallel", "arbitrary")),
        name="conv_branch",
    )(u_c, u_c, u_c, conv_w)


def _merge_kernel(ya_ref, yb_ref, yc_ref, g0_ref, g1_ref, g2_ref, w_ref, o_ref):
    acc = g0_ref[...].astype(F32) * _dot(ya_ref[...], w_ref[0])
    acc = acc + g1_ref[...].astype(F32) * _dot(yb_ref[...], w_ref[1])
    acc = acc + g2_ref[...].astype(F32) * _dot(yc_ref[...], w_ref[2])
    o_ref[...] = acc.astype(o_ref.dtype)


def merge_branches(ya, yb, yc, gates, w_branch):
    m = ya.shape[0]
    bm = _pick(m, 1024)
    bn = 1024
    nj = D_MODEL // bn
    yspec = pl.BlockSpec((bm, WIDTH), lambda j, i: (i, 0))
    gspecs = [pl.BlockSpec((bm, bn), functools.partial(lambda j, i, t: (i, t * nj + j), t=t))
              for t in range(3)]
    return pl.pallas_call(
        _merge_kernel,
        grid=(nj, m // bm),
        in_specs=[yspec] * 3 + gspecs + [pl.BlockSpec((3, WIDTH, bn), lambda j, i: (0, 0, j))],
        out_specs=pl.BlockSpec((bm, bn), lambda j, i: (i, j)),
        out_shape=jax.ShapeDtypeStruct((m, D_MODEL), BF16),
        compiler_params=_cparams(("arbitrary", "arbitrary")),
        name="merge",
    )(ya, yb, yc, gates, gates, gates, w_branch)


def _out_proj_ln_kernel(x_ref, w_ref, res_ref, g_ref, b_ref, o_ref, ob_ref):
    y = _ln_math(ALPHA * res_ref[...] + _dot(x_ref[...], w_ref[...]), g_ref[...], b_ref[...])
    o_ref[...] = y
    ob_ref[...] = y.astype(BF16)


def out_proj_ln(x, w, res, g, b):
    m, k = x.shape
    n = w.shape[1]
    bm = _pick(m, 512)
    row = pl.BlockSpec((bm, n), lambda i: (i, 0))
    vec = pl.BlockSpec((1, n), lambda i: (0, 0))
    return pl.pallas_call(
        _out_proj_ln_kernel,
        grid=(m // bm,),
        in_specs=[pl.BlockSpec((bm, k), lambda i: (i, 0)), pl.BlockSpec((k, n), lambda i: (0, 0)),
                  row, vec, vec],
        out_specs=[row, row],
        out_shape=[jax.ShapeDtypeStruct((m, n), F32), jax.ShapeDtypeStruct((m, n), BF16)],
        compiler_params=_cparams(("parallel",)),
        name="out_proj_ln",
    )(x, w, res, g, b)


def _ffn_kernel(x_ref, wg_ref, wu_ref, wo_ref, res_ref, g_ref, b_ref, o_ref, ob_ref, acc_ref):
    j = pl.program_id(1)

    @pl.when(j == 0)
    def _():
        acc_ref[...] = jnp.zeros(acc_ref.shape, F32)

    x = x_ref[...]
    gate = _dot(x, wg_ref[...])
    up = _dot(x, wu_ref[...])
    f = (gate * jax.nn.sigmoid(gate) * up).astype(BF16)
    acc_ref[...] += _dot(f, wo_ref[...])

    @pl.when(j == pl.num_programs(1) - 1)
    def _():
        y = _ln_math(ALPHA * res_ref[...] + acc_ref[...], g_ref[...], b_ref[...])
        o_ref[...] = y
        ob_ref[...] = y.astype(BF16)


def ffn(x, w_in, w_out, res, g, b):
    m, k = x.shape
    n = w_out.shape[1]
    bm = _pick(m, 512)
    bn = 512
    nj = D_FF // bn
    row = pl.BlockSpec((bm, n), lambda i, j: (i, 0))
    vec = pl.BlockSpec((1, n), lambda i, j: (0, 0))
    return pl.pallas_call(
        _ffn_kernel,
        grid=(m // bm, nj),
        in_specs=[pl.BlockSpec((bm, k), lambda i, j: (i, 0)),
                  pl.BlockSpec((k, bn), lambda i, j: (0, j)),
                  pl.BlockSpec((k, bn), lambda i, j: (0, nj + j)),
                  pl.BlockSpec((bn, n), lambda i, j: (j, 0)),
                  row, vec, vec],
        out_specs=[row, row],
        out_shape=[jax.ShapeDtypeStruct((m, n), F32), jax.ShapeDtypeStruct((m, n), BF16)],
        scratch_shapes=[pltpu.VMEM((bm, n), F32)],
        compiler_params=_cparams(("parallel", "arbitrary")),
        name="ffn",
    )(x, w_in, w_in, w_out, res, g, b)


def _prep_params(w_in, q_norm_g, w_uq, kv_norm_g, w_ukv, rwkv_mu, rwkv_w0, rwkv_w_up, rwkv_a0,
                 rwkv_a_up, rwkv_g_up, rwkv_k_k, rwkv_k_a, rwkv_r_k, rwkv_lnx_g, rwkv_lnx_b,
                 conv_w, w_branch, w_out, ln1_g, ln1_b, w_ffn_in, w_ffn_out, ln2_g, ln2_b):
    nl = w_in.shape[0]
    z = lambda *s: jnp.zeros(s, F32)
    o = MLA_COLS
    kr = w_in[:, :, o - ROPE:o]
    kr_rot = jnp.concatenate([-kr[..., ROPE // 2:], kr[..., :ROPE // 2]], axis=-1)
    w_mla = jnp.concatenate([w_in[:, :, :o], z(nl, D_MODEL, 64), kr_rot, z(nl, D_MODEL, 64)], axis=-1)
    w_rw = jnp.concatenate([w_in[:, :, o:o + RWKV_COLS], z(nl, D_MODEL, RWKV_PAD - RWKV_COLS)], axis=-1)
    o += RWKV_COLS
    w_cv = w_in[:, :, o:o + CONV_COLS]
    o += CONV_COLS
    w_gt = w_in[:, :, o:]

    wq = w_uq.reshape(nl, Q_LORA, MLA_HEADS, NOPE + ROPE)
    x1 = wq[..., NOPE:NOPE + ROPE // 2]
    x2 = wq[..., NOPE + ROPE // 2:]
    pad = z(nl, Q_LORA, MLA_HEADS, 128 - ROPE)
    wq_all = jnp.concatenate([wq[..., :NOPE].reshape(nl, Q_LORA, WIDTH),
                              jnp.concatenate([x1, x2, pad], -1).reshape(nl, Q_LORA, WIDTH),
                              jnp.concatenate([-x2, x1, pad], -1).reshape(nl, Q_LORA, WIDTH)], axis=-1)
    wkv = w_ukv.reshape(nl, KV_LORA, MLA_HEADS, NOPE + VDIM)
    wkv_all = jnp.concatenate([wkv[..., :NOPE].reshape(nl, KV_LORA, WIDTH),
                               wkv[..., NOPE:].reshape(nl, KV_LORA, WIDTH)], axis=-1)

    zl = z(nl, DECAY_LORA, WIDTH)
    wup = jnp.concatenate([jnp.concatenate([rwkv_w_up[:, 0], zl], -1),
                           jnp.concatenate([zl, rwkv_w_up[:, 1]], -1)], axis=1)
    aup = jnp.concatenate([jnp.concatenate([rwkv_a_up[:, 0], zl], -1),
                           jnp.concatenate([zl, rwkv_a_up[:, 1]], -1)], axis=1)
    gup = jnp.concatenate([rwkv_g_up, z(nl, GD_PAD - GATE_LORA, WIDTH)], axis=1)
    mu = jnp.concatenate([rwkv_mu, z(nl, 2, RWKV_PAD - RWKV_COLS)], axis=-1)
    head = jnp.arange(SEG_LANES) // RW_N
    ones_bd = (head[:, None] == head[None, :]).astype(BF16)
    row = lambda t: t.reshape(nl, 1, -1)
    return dict(
        w_mla=w_mla.astype(BF16), w_rw=w_rw.astype(BF16), w_cv=w_cv.astype(BF16), w_gt=w_gt.astype(BF16),
        qg=row(q_norm_g), kvg=row(kv_norm_g), wq=wq_all.astype(BF16), wkv=wkv_all.astype(BF16),
        mu=mu, w0=row(rwkv_w0), a0=row(rwkv_a0), wup=wup.astype(BF16), aup=aup.astype(BF16),
        gup=gup.astype(BF16), k_k=row(rwkv_k_k), k_a=row(rwkv_k_a), r_k=row(rwkv_r_k),
        lnx_g=row(rwkv_lnx_g), lnx_b=row(rwkv_lnx_b), conv_w=conv_w,
        w_branch=w_branch.astype(BF16), w_out=w_out.astype(BF16),
        ln1_g=row(ln1_g), ln1_b=row(ln1_b), w_ffn_in=w_ffn_in.astype(BF16),
        w_ffn_out=w_ffn_out.astype(BF16), ln2_g=row(ln2_g), ln2_b=row(ln2_b),
    ), ones_bd


def _rope_tables(seq):
    pos = jnp.arange(seq, dtype=F32)
    inv = ROPE_THETA ** (-jnp.arange(0, ROPE, 2, dtype=F32) / ROPE)
    ang = pos[:, None] * inv[None, :]
    c, s = jnp.cos(ang), jnp.sin(ang)
    zpad = jnp.zeros((seq, 128 - ROPE), F32)
    return jnp.concatenate([c, c, zpad], -1), jnp.concatenate([s, s, zpad], -1)


def _layer(x, xb, p, ones_bd, cosz, sinz, bsz, seq):
    u_a = matmul(xb, p["w_mla"], MLA_PAD, name="in_mla")
    u_b = matmul(xb, p["w_rw"], RWKV_PAD // 2, name="in_rwkv")
    u_c = matmul(xb, p["w_cv"], 1024, out_dtype=BF16, name="in_conv")
    gates = matmul(xb, p["w_gt"], 1024, out_dtype=BF16, name="in_gate", sigmoid=True)

    q, k, v = mla_prep(u_a, p["qg"], p["kvg"], p["wq"], p["wkv"], cosz, sinz, seq)
    y_a = attention(q, k, v, bsz, seq)

    pp = dict(p, ones_bd=ones_bd)
    r, vv, a, lw0, lw1, k0, k1, b0, b1, bonus, g = rwkv_prep(u_b, pp, bsz, seq)
    y0, y1 = rwkv_scan(r, vv, a, lw0, k0, b0, lw1, k1, b1, bsz, seq)
    y_b = rwkv_post(y0, y1, bonus, g, p["lnx_g"], p["lnx_b"], ones_bd)

    y_c = conv_branch(u_c, p["conv_w"], bsz, seq)

    merged = merge_branches(y_a, y_b, y_c, gates, p["w_branch"])
    h, hb = out_proj_ln(merged, p["w_out"], x, p["ln1_g"], p["ln1_b"])
    return ffn(hb, p["w_ffn_in"], p["w_ffn_out"], h, p["ln2_g"], p["ln2_b"])


def _trunk(x3, ln_g, ln_b, params, ones_bd):
    bsz, seq, d = x3.shape
    cosz, sinz = _rope_tables(seq)
    x, xb = layer_norm_in(x3.reshape(bsz * seq, d), ln_g.reshape(1, d), ln_b.reshape(1, d))
    for l in range(DEPTH):
        p = {name: t[l] for name, t in params.items()}
        x, xb = _layer(x, xb, p, ones_bd, cosz, sinz, bsz, seq)
    return x.reshape(bsz, seq, d)


def kernel(x_prompt, x_sample, ln_in_g, ln_in_b, w_in, q_norm_g, w_uq, kv_norm_g, w_ukv, rwkv_mu, rwkv_w0, rwkv_w_up, rwkv_a0, rwkv_a_up, rwkv_g_up, rwkv_k_k, rwkv_k_a, rwkv_r_k, rwkv_lnx_g, rwkv_lnx_b, conv_w, w_branch, w_out, ln1_g, ln1_b, w_ffn_in, w_ffn_out, ln2_g, ln2_b):
    params, ones_bd = _prep_params(w_in, q_norm_g, w_uq, kv_norm_g, w_ukv, rwkv_mu, rwkv_w0, rwkv_w_up,
                                   rwkv_a0, rwkv_a_up, rwkv_g_up, rwkv_k_k, rwkv_k_a, rwkv_r_k,
                                   rwkv_lnx_g, rwkv_lnx_b, conv_w, w_branch, w_out, ln1_g, ln1_b,
                                   w_ffn_in, w_ffn_out, ln2_g, ln2_b)
    y_prompt = _trunk(x_prompt, ln_in_g, ln_in_b, params, ones_bd)
    y_sample = _trunk(x_sample, ln_in_g, ln_in_b, params, ones_bd)
    return (y_prompt, y_sample)
```

```python
import functools

import jax
import jax.numpy as jnp
from jax import lax
from jax.experimental import pallas as pl
from jax.experimental.pallas import tpu as pltpu

F32 = jnp.float32
BF16 = jnp.bfloat16

D_MODEL = 2048
DEPTH = 4
WIDTH = D_MODEL // 2
MLA_HEADS = 8
NOPE = 128
ROPE = 64
VDIM = 128
Q_LORA = 768
KV_LORA = 512
ROPE_THETA = 10000.0
RW_HEADS = 16
RW_N = 64
DECAY_LORA = 64
AAA_LORA = 64
GATE_LORA = 160
D_FF = 5632
LN_EPS = 1e-5
RMS_EPS = 1e-6
LNX_EPS = 1e-5 * RW_N
ALPHA = (2 * DEPTH) ** 0.25
LOG2_E = 1.4426950408889634
EXP_NEG_HALF = 0.6065306597126334
MLA_COLS = Q_LORA + KV_LORA + ROPE
RWKV_COLS = 3 * WIDTH + 2 * DECAY_LORA + 2 * AAA_LORA + GATE_LORA
CONV_COLS = 3 * WIDTH
GATE_COLS = 3 * D_MODEL
MLA_PAD = 1536
RWKV_PAD = 3584
GD_PAD = 256
QK_PAD = 256
CHUNK = 64
CONV_HALO = 16
SEG_LANES = 256

VMEM_LIMIT = 56 * 1024 * 1024


def _cparams(sem):
    return pltpu.CompilerParams(dimension_semantics=sem, vmem_limit_bytes=VMEM_LIMIT)


def _dot(a, b):
    return jnp.dot(a, b, preferred_element_type=F32)


def _dot_nt(a, b):
    return lax.dot_general(a, b, (((1,), (1,)), ((), ())), preferred_element_type=F32)


def _dot_tn(a, b):
    return lax.dot_general(a, b, (((0,), (0,)), ((), ())), preferred_element_type=F32)


def _split2(x):
    hi = x.astype(BF16)
    lo = (x - hi.astype(F32)).astype(BF16)
    return hi, lo


def _split3(x):
    hi = x.astype(BF16)
    r1 = x - hi.astype(F32)
    mid = r1.astype(BF16)
    lo = (r1 - mid.astype(F32)).astype(BF16)
    return hi, mid, lo


def _seg_sum(x, ones_bd):
    hi, lo = _split2(x)
    parts = []
    for c in range(0, x.shape[1], SEG_LANES):
        parts.append(_dot(hi[:, c:c + SEG_LANES], ones_bd) + _dot(lo[:, c:c + SEG_LANES], ones_bd))
    return jnp.concatenate(parts, axis=1)


def _pick(n, target):
    b = min(n, target)
    while n % b:
        b //= 2
    return b


def _ln_math(x, g, b):
    mu = jnp.mean(x, axis=-1, keepdims=True)
    xc = x - mu
    var = jnp.mean(xc * xc, axis=-1, keepdims=True)
    return xc * lax.rsqrt(var + LN_EPS) * g + b


def _ln_kernel(x_ref, g_ref, b_ref, o_ref, ob_ref):
    y = _ln_math(x_ref[...], g_ref[...], b_ref[...])
    o_ref[...] = y
    ob_ref[...] = y.astype(BF16)


def layer_norm_in(x, g, b):
    m, d = x.shape
    bm = _pick(m, 512)
    return pl.pallas_call(
        _ln_kernel,
        grid=(m // bm,),
        in_specs=[pl.BlockSpec((bm, d), lambda i: (i, 0)),
                  pl.BlockSpec((1, d), lambda i: (0, 0)),
                  pl.BlockSpec((1, d), lambda i: (0, 0))],
        out_specs=[pl.BlockSpec((bm, d), lambda i: (i, 0)),
                   pl.BlockSpec((bm, d), lambda i: (i, 0))],
        out_shape=[jax.ShapeDtypeStruct((m, d), F32), jax.ShapeDtypeStruct((m, d), BF16)],
        compiler_params=_cparams(("parallel",)),
        name="ln_in",
    )(x, g, b)


def _mm_kernel(x_ref, w_ref, o_ref, *, sigmoid):
    x = x_ref[...]
    if sigmoid:
        half = w_ref.shape[1] // 2
        for c in (0, half):
            o_ref[:, c:c + half] = jax.nn.sigmoid(_dot(x, w_ref[:, c:c + half])).astype(o_ref.dtype)
    else:
        o_ref[...] = _dot(x, w_ref[...]).astype(o_ref.dtype)


def matmul(x, w, bn, out_dtype=F32, name="mm", sigmoid=False):
    m, k = x.shape
    n = w.shape[1]
    bm = _pick(m, 1024)
    return pl.pallas_call(
        functools.partial(_mm_kernel, sigmoid=sigmoid),
        grid=(m // bm, n // bn),
        in_specs=[pl.BlockSpec((bm, k), lambda i, j: (i, 0)),
                  pl.BlockSpec((k, bn), lambda i, j: (0, j))],
        out_specs=pl.BlockSpec((bm, bn), lambda i, j: (i, j)),
        out_shape=jax.ShapeDtypeStruct((m, n), out_dtype),
        compiler_params=_cparams(("parallel", "arbitrary")),
        name=name,
    )(x, w)


def _rms(x, g):
    return x * lax.rsqrt(jnp.mean(x * x, axis=-1, keepdims=True) + RMS_EPS) * g


def _mla_prep_kernel(u_ref, qg_ref, kvg_ref, wq_ref, wkv_ref, cos_ref, sin_ref,
                     q_ref, k_ref, v_ref):
    scale = (NOPE + ROPE) ** -0.5 * LOG2_E
    cz = cos_ref[...]
    sz = sin_ref[...]
    cq = _rms(u_ref[:, :Q_LORA], qg_ref[...]).astype(BF16)
    pq = _dot(cq, wq_ref[...])
    ckv = _rms(u_ref[:, Q_LORA:Q_LORA + KV_LORA], kvg_ref[...]).astype(BF16)
    pkv = _dot(ckv, wkv_ref[...])
    o = Q_LORA + KV_LORA
    krz = (u_ref[:, o:o + 128] * cz + u_ref[:, o + 128:o + 256] * sz).astype(BF16)
    lane = lax.broadcasted_iota(jnp.int32, (cz.shape[0], 128), 1)
    one_col = jnp.where(lane == 0, 1.0, 0.0).astype(BF16)
    for h in range(MLA_HEADS):
        a, b = h * 128, (h + 1) * 128
        q_ref[:, h * QK_PAD:h * QK_PAD + 128] = (pq[:, a:b] * scale).astype(BF16)
        q_ref[:, h * QK_PAD + 128:(h + 1) * QK_PAD] = (
            (pq[:, WIDTH + a:WIDTH + b] * cz + pq[:, 2 * WIDTH + a:2 * WIDTH + b] * sz) * scale
        ).astype(BF16)
        k_ref[:, h * QK_PAD:h * QK_PAD + 128] = pkv[:, a:b].astype(BF16)
        k_ref[:, h * QK_PAD + 128:(h + 1) * QK_PAD] = krz
        v_ref[:, h * QK_PAD:h * QK_PAD + 128] = pkv[:, WIDTH + a:WIDTH + b].astype(BF16)
        v_ref[:, h * QK_PAD + 128:(h + 1) * QK_PAD] = one_col


def mla_prep(u_a, qg, kvg, wq, wkv, cosz, sinz, seq):
    m = u_a.shape[0]
    bm = _pick(seq, 512)
    nb = seq // bm
    full = lambda i: (0, 0)
    return pl.pallas_call(
        _mla_prep_kernel,
        grid=(m // bm,),
        in_specs=[pl.BlockSpec((bm, MLA_PAD), lambda i: (i, 0)),
                  pl.BlockSpec((1, Q_LORA), full),
                  pl.BlockSpec((1, KV_LORA), full),
                  pl.BlockSpec(wq.shape, full),
                  pl.BlockSpec(wkv.shape, full),
                  pl.BlockSpec((bm, 128), lambda i: (i % nb, 0)),
                  pl.BlockSpec((bm, 128), lambda i: (i % nb, 0))],
        out_specs=[pl.BlockSpec((bm, MLA_HEADS * QK_PAD), lambda i: (i, 0)),
                   pl.BlockSpec((bm, MLA_HEADS * QK_PAD), lambda i: (i, 0)),
                   pl.BlockSpec((bm, MLA_HEADS * QK_PAD), lambda i: (i, 0))],
        out_shape=[jax.ShapeDtypeStruct((m, MLA_HEADS * QK_PAD), BF16)] * 3,
        compiler_params=_cparams(("parallel",)),
        name="mla_prep",
    )(u_a, qg, kvg, wq, wkv, cosz, sinz)


def _attn_kernel(q_ref, k_ref, v_ref, o_ref, *, nkv, bk):
    q = q_ref[...]
    m = jnp.full((q.shape[0], 1), -jnp.inf, F32)
    acc = jnp.zeros((q.shape[0], QK_PAD), F32)
    s = _dot_nt(q, k_ref[0:bk, :])
    for j in range(nkv):
        if j + 1 < nkv:
            s_next = _dot_nt(q, k_ref[(j + 1) * bk:(j + 2) * bk, :])
        m_new = jnp.maximum(m, jnp.max(s, axis=-1, keepdims=True))
        alpha = jnp.exp2(m - m_new)
        p = jnp.exp2((s - m_new).astype(BF16))
        acc = alpha * acc + _dot(p, v_ref[j * bk:(j + 1) * bk, :])
        m = m_new
        if j + 1 < nkv:
            s = s_next
    o_ref[...] = (acc[:, :VDIM] / acc[:, VDIM:VDIM + 1]).astype(o_ref.dtype)


def attention(q, k, v, bsz, seq):
    m = q.shape[0]
    bq = _pick(seq, 1024)
    bk = _pick(seq, 1024)
    nq = seq // bq
    return pl.pallas_call(
        functools.partial(_attn_kernel, nkv=seq // bk, bk=bk),
        grid=(bsz, MLA_HEADS, nq),
        in_specs=[pl.BlockSpec((bq, QK_PAD), lambda b, h, i: (b * nq + i, h)),
                  pl.BlockSpec((seq, QK_PAD), lambda b, h, i: (b, h)),
                  pl.BlockSpec((seq, QK_PAD), lambda b, h, i: (b, h))],
        out_specs=pl.BlockSpec((bq, VDIM), lambda b, h, i: (b * nq + i, h)),
        out_shape=jax.ShapeDtypeStruct((m, WIDTH), BF16),
        compiler_params=_cparams(("parallel", "parallel", "arbitrary")),
        name="mla_attn",
    )(q, k, v)


def _shifted(u, prev_row, next_row):
    n = u.shape[0]
    row = lax.broadcasted_iota(jnp.int32, (n, 1), 0)
    up = jnp.where(row == 0, prev_row, pltpu.roll(u, 1, axis=0))
    un = jnp.where(row == n - 1, next_row, pltpu.roll(u, n - 1, axis=0))
    return up, un


def _halo_specs(bm, width, nb, bsz, hr):
    per = bm // hr
    last = bsz * nb * per - 1
    cur = pl.BlockSpec((bm, width), lambda b, i: (b * nb + i, 0))
    prev = pl.BlockSpec((hr, width), lambda b, i: (jnp.maximum((b * nb + i) * per - 1, 0), 0))
    nxt = pl.BlockSpec((hr, width), lambda b, i: (jnp.minimum((b * nb + i + 1) * per, last), 0))
    return cur, prev, nxt


def _rwkv_prep_kernel(u_ref, up_ref, un_ref, mu_ref, w0_ref, a0_ref, wup_ref, aup_ref, gup_ref,
                      kk_ref, ka_ref, rk_ref, ones_ref,
                      r_o, v_o, a_o, lw0_o, lw1_o, k0_o, k1_o, b0_o, b1_o, bonus_o, g_o):
    i = pl.program_id(1)
    u = u_ref[...]
    prev_row = jnp.where(i == 0, 0.0, up_ref[7:8, :])
    next_row = jnp.where(i == pl.num_programs(1) - 1, 0.0, un_ref[0:1, :])
    up, un = _shifted(u, prev_row, next_row)
    us = u + mu_ref[0:1, :] * (up - u) + mu_ref[1:2, :] * (un - u)
    r = us[:, 0:WIDTH]
    k = us[:, WIDTH:2 * WIDTH]
    v = us[:, 2 * WIDTH:3 * WIDTH]
    o = 3 * WIDTH
    wd = us[:, o:o + 2 * DECAY_LORA]
    ad = us[:, o + 128:o + 128 + 2 * AAA_LORA]
    gd = us[:, o + 256:o + 256 + GD_PAD]
    g_o[...] = _dot(jax.nn.sigmoid(gd).astype(BF16), gup_ref[...]).astype(g_o.dtype)
    t = w0_ref[...] + _dot(jnp.tanh(wd).astype(BF16), wup_ref[...])
    lw = -EXP_NEG_HALF * jax.nn.sigmoid(t)
    al = jax.nn.sigmoid(a0_ref[...] + _dot(ad.astype(BF16), aup_ref[...]))
    ones_bd = ones_ref[...]
    kkr = k * kk_ref[...]
    kk = kkr * lax.rsqrt(jnp.maximum(_seg_sum(kkr * kkr, ones_bd), 1e-24))
    ka = ka_ref[...]
    a_0 = al[:, :WIDTH]
    a_1 = al[:, WIDTH:]
    k_0 = k * (1.0 + (a_0 - 1.0) * ka)
    k_1 = k * (1.0 + (a_1 - 1.0) * ka)
    r_o[...] = r.astype(r_o.dtype)
    v_o[...] = v.astype(v_o.dtype)
    a_o[...] = (-kk).astype(a_o.dtype)
    lw0_o[...] = lw[:, :WIDTH]
    lw1_o[...] = lw[:, WIDTH:]
    k0_o[...] = k_0.astype(k0_o.dtype)
    k1_o[...] = k_1.astype(k1_o.dtype)
    b0_o[...] = (kk * a_0).astype(b0_o.dtype)
    b1_o[...] = (kk * a_1).astype(b1_o.dtype)
    bonus_o[...] = (_seg_sum(r * rk_ref[...] * (k_0 + k_1), ones_bd) * v).astype(bonus_o.dtype)


def rwkv_prep(u_b, p, bsz, seq):
    m = u_b.shape[0]
    bm = _pick(seq, 128)
    nb = seq // bm
    cur, prev, nxt = _halo_specs(bm, RWKV_PAD, nb, bsz, 8)
    full = lambda b, i: (0, 0)
    consts = [p["mu"], p["w0"], p["a0"], p["wup"], p["aup"], p["gup"], p["k_k"], p["k_a"], p["r_k"],
              p["ones_bd"]]
    out_spec = pl.BlockSpec((bm, WIDTH), lambda b, i: (b * nb + i, 0))
    return pl.pallas_call(
        _rwkv_prep_kernel,
        grid=(bsz, nb),
        in_specs=[cur, prev, nxt] + [pl.BlockSpec(c.shape, full) for c in consts],
        out_specs=[out_spec] * 11,
        out_shape=[jax.ShapeDtypeStruct((m, WIDTH), dt)
                   for dt in (BF16, BF16, BF16, F32, F32, BF16, BF16, BF16, BF16, BF16, BF16)],
        compiler_params=_cparams(("parallel", "arbitrary")),
        name="rwkv_prep",
    )(u_b, u_b, u_b, *consts)


def _scan_chunk_operands(refs, rows, reverse):
    r_ref, lw_ref, k_ref, v_ref, a_ref, b_ref = refs
    ti = lax.broadcasted_iota(jnp.int32, (CHUNK, CHUNK), 0)
    si = lax.broadcasted_iota(jnp.int32, (CHUNK, CHUNK), 1)
    tri = ((si >= ti) if reverse else (si <= ti)).astype(BF16)
    last = 0 if reverse else CHUNK - 1
    lw = lw_ref[rows, :]
    l_hi, l_mid, l_lo = _split3(lw)
    cum = _dot(tri, l_hi) + _dot(tri, l_mid) + _dot(tri, l_lo)
    gam = jnp.exp(cum)
    ginv = jnp.exp(-cum)
    rt = r_ref[rows, :].astype(F32) * gam
    at = a_ref[rows, :].astype(F32) * jnp.exp(cum - lw)
    kt = k_ref[rows, :].astype(F32) * ginv
    bt = b_ref[rows, :].astype(F32) * ginv
    v = v_ref[rows, :].astype(F32)
    g_end = gam[last:last + 1, :]
    out = []
    for h in range(RW_HEADS):
        sl = slice(h * RW_N, (h + 1) * RW_N)
        ar = jnp.concatenate([at[:, sl].astype(BF16), rt[:, sl].astype(BF16)], axis=0)
        bk = jnp.concatenate([bt[:, sl].astype(BF16), kt[:, sl].astype(BF16)], axis=0)
        out.append((ar, bk, v[:, sl].astype(BF16), g_end[:, sl]))
    return out


def _scan_kernel(rf_ref, lwf_ref, kf_ref, vf_ref, af_ref, bf_ref,
                 rb_ref, lwb_ref, kb_ref, vb_ref, ab_ref, bb_ref,
                 yf_ref, yb_ref, s_ref, *, nchunk):
    @pl.when(pl.program_id(1) == 0)
    def _():
        s_ref[...] = jnp.zeros(s_ref.shape, F32)

    ti = lax.broadcasted_iota(jnp.int32, (CHUNK, 2 * CHUNK), 0)
    lane = lax.broadcasted_iota(jnp.int32, (CHUNK, 2 * CHUNK), 1)
    si = lane & (CHUNK - 1)
    left = lane < CHUNK
    masks = []
    for reverse in (False, True):
        strict = (si > ti) if reverse else (si < ti)
        incl = (si >= ti) if reverse else (si <= ti)
        masks += [(strict, incl)] * RW_HEADS
    n = 2 * RW_HEADS
    streams = range(n)

    def chunk(ci, carry):
        rows_f = pl.ds(pl.multiple_of(ci * CHUNK, CHUNK), CHUNK)
        rows_b = pl.ds(pl.multiple_of((nchunk - 1 - ci) * CHUNK, CHUNK), CHUNK)
        ops = (_scan_chunk_operands((rf_ref, lwf_ref, kf_ref, vf_ref, af_ref, bf_ref), rows_f, False)
               + _scan_chunk_operands((rb_ref, lwb_ref, kb_ref, vb_ref, ab_ref, bb_ref), rows_b, True))
        ar = [o[0] for o in ops]
        bk = [o[1] for o in ops]
        v_h = [o[2] for o in ops]
        g_end = [o[3] for o in ops]
        s0 = [s_ref[i] for i in streams]
        g = [_dot_nt(ar[i], bk[i]) for i in streams]
        hs = [_dot_nt(ar[i], s0[i].astype(BF16)) for i in streams]
        g_top = [jnp.where(masks[i][0], g[i][:CHUNK], 0.0) for i in streams]
        g_bot = [jnp.where(masks[i][1], g[i][CHUNK:], 0.0).astype(BF16) for i in streams]
        w_in = [hs[i][:CHUNK] + _dot(g_top[i][:, CHUNK:].astype(BF16), v_h[i]) for i in streams]
        zero = jnp.zeros((CHUNK, CHUNK), F32)
        z = [jnp.where(left, g_top[i], jnp.concatenate([zero, w_in[i]], axis=1)) for i in streams]
        for _ in range(CHUNK.bit_length() - 1):
            r = [_dot(z[i][:, :CHUNK].astype(BF16), z[i].astype(BF16)) for i in streams]
            z = [r[i] + jnp.where(left, 0.0, z[i]) for i in streams]
        uv = [jnp.concatenate([z[i][:, CHUNK:].astype(BF16), v_h[i]], axis=0) for i in streams]
        y = [hs[i][CHUNK:] + _dot(g_bot[i], uv[i]) for i in streams]
        ds = [_dot_tn(uv[i], bk[i]) for i in streams]
        for i in streams:
            s_ref[i] = (s0[i] + ds[i]) * g_end[i]
            h = i % RW_HEADS
            sl = slice(h * RW_N, (h + 1) * RW_N)
            if i < RW_HEADS:
                yf_ref[rows_f, sl] = y[i]
            else:
                yb_ref[rows_b, sl] = y[i]
        return carry

    lax.fori_loop(0, nchunk, chunk, 0)


def rwkv_scan(r, v, a, lw0, k0, b0, lw1, k1, b1, bsz, seq):
    m = r.shape[0]
    bt = _pick(seq, 256)
    nb = seq // bt
    fwd = pl.BlockSpec((bt, WIDTH), lambda bi, i: (bi * nb + i, 0))
    bwd = pl.BlockSpec((bt, WIDTH), lambda bi, i: (bi * nb + nb - 1 - i, 0))
    return pl.pallas_call(
        functools.partial(_scan_kernel, nchunk=bt // CHUNK),
        grid=(bsz, nb),
        in_specs=[fwd] * 6 + [bwd] * 6,
        out_specs=[fwd, bwd],
        out_shape=[jax.ShapeDtypeStruct((m, WIDTH), F32)] * 2,
        scratch_shapes=[pltpu.VMEM((2 * RW_HEADS, RW_N, RW_N), F32)],
        compiler_params=_cparams(("parallel", "arbitrary")),
        name="rwkv_scan",
    )(r, lw0, k0, v, a, b0, r, lw1, k1, v, a, b1)


def _rwkv_post_kernel(y0_ref, y1_ref, bonus_ref, g_ref, lg_ref, lb_ref, ones_ref, o_ref):
    ones_bd = ones_ref[...]
    y = y0_ref[...] + y1_ref[...]
    mu = _seg_sum(y, ones_bd) * (1.0 / RW_N)
    yc = y - mu
    var = _seg_sum(yc * yc, ones_bd) * (1.0 / RW_N)
    yn = yc * lax.rsqrt(var + LNX_EPS) * lg_ref[...] + lb_ref[...]
    o_ref[...] = ((yn + bonus_ref[...]) * g_ref[...]).astype(o_ref.dtype)


def rwkv_post(y0, y1, bonus, g, lg, lb, ones_bd):
    m = y0.shape[0]
    bm = _pick(m, 512)
    spec = pl.BlockSpec((bm, WIDTH), lambda i: (i, 0))
    full = lambda i: (0, 0)
    return pl.pallas_call(
        _rwkv_post_kernel,
        grid=(m // bm,),
        in_specs=[spec] * 4 + [pl.BlockSpec((1, WIDTH), full), pl.BlockSpec((1, WIDTH), full),
                               pl.BlockSpec((SEG_LANES, SEG_LANES), full)],
        out_specs=spec,
        out_shape=jax.ShapeDtypeStruct((m, WIDTH), BF16),
        compiler_params=_cparams(("parallel",)),
        name="rwkv_post",
    )(y0, y1, bonus, g, lg, lb, ones_bd)


def _conv_kernel(u_ref, up_ref, un_ref, w_ref, o_ref):
    i = pl.program_id(1)
    u = u_ref[...].astype(F32)
    hp = u[:, WIDTH:2 * WIDTH] * u[:, 2 * WIDTH:]
    pr = up_ref[...].astype(F32)[CONV_HALO - 1:CONV_HALO, :]
    nr = un_ref[...].astype(F32)[0:1, :]
    prev_row = jnp.where(i == 0, 0.0, pr[:, WIDTH:2 * WIDTH] * pr[:, 2 * WIDTH:])
    next_row = jnp.where(i == pl.num_programs(1) - 1, 0.0, nr[:, WIDTH:2 * WIDTH] * nr[:, 2 * WIDTH:])
    hm, hn = _shifted(hp, prev_row, next_row)
    conv = w_ref[0:1, :] * hm + w_ref[1:2, :] * hp + w_ref[2:3, :] * hn
    o_ref[...] = (u[:, :WIDTH] * conv).astype(o_ref.dtype)


def conv_branch(u_c, conv_w, bsz, seq):
    m = u_c.shape[0]
    bm = _pick(seq, 512)
    nb = seq // bm
    cur, prev, nxt = _halo_specs(bm, CONV_COLS, nb, bsz, CONV_HALO)
    return pl.pallas_call(
        _conv_kernel,
        grid=(bsz, nb),
        in_specs=[cur, prev, nxt, pl.BlockSpec((3, WIDTH), lambda b, i: (0, 0))],
        out_specs=pl.BlockSpec((bm, WIDTH), lambda b, i: (b * nb + i, 0)),
        out_shape=jax.ShapeDtypeStruct((m, WIDTH), BF16),
        compiler_params=_cparams(("parallel", "arbitrary")),
        name="conv_branch",
    )(u_c, u_c, u_c, conv_w)


def _merge_kernel(ya_ref, yb_ref, yc_ref, g0_ref, g1_ref, g2_ref, w_ref, o_ref):
    acc = g0_ref[...].astype(F32) * _dot(ya_ref[...], w_ref[0])
    acc = acc + g1_ref[...].astype(F32) * _dot(yb_ref[...], w_ref[1])
    acc = acc + g2_ref[...].astype(F32) * _dot(yc_ref[...], w_ref[2])
    o_ref[...] = acc.astype(o_ref.dtype)


def merge_branches(ya, yb, yc, gates, w_branch):
    m = ya.shape[0]
    bm = _pick(m, 1024)
    bn = 1024
    nj = D_MODEL // bn
    yspec = pl.BlockSpec((bm, WIDTH), lambda j, i: (i, 0))
    gspecs = [pl.BlockSpec((bm, bn), functools.partial(lambda j, i, t: (i, t * nj + j), t=t))
              for t in range(3)]
    return pl.pallas_call(
        _merge_kernel,
        grid=(nj, m // bm),
        in_specs=[yspec] * 3 + gspecs + [pl.BlockSpec((3, WIDTH, bn), lambda j, i: (0, 0, j))],
        out_specs=pl.BlockSpec((bm, bn), lambda j, i: (i, j)),
        out_shape=jax.ShapeDtypeStruct((m, D_MODEL), BF16),
        compiler_params=_cparams(("arbitrary", "arbitrary")),
        name="merge",
    )(ya, yb, yc, gates, gates, gates, w_branch)


def _out_proj_ln_kernel(x_ref, w_ref, res_ref, g_ref, b_ref, o_ref, ob_ref):
    y = _ln_math(ALPHA * res_ref[...] + _dot(x_ref[...], w_ref[...]), g_ref[...], b_ref[...])
    o_ref[...] = y
    ob_ref[...] = y.astype(BF16)


def out_proj_ln(x, w, res, g, b):
    m, k = x.shape
    n = w.shape[1]
    bm = _pick(m, 512)
    row = pl.BlockSpec((bm, n), lambda i: (i, 0))
    vec = pl.BlockSpec((1, n), lambda i: (0, 0))
    return pl.pallas_call(
        _out_proj_ln_kernel,
        grid=(m // bm,),
        in_specs=[pl.BlockSpec((bm, k), lambda i: (i, 0)), pl.BlockSpec((k, n), lambda i: (0, 0)),
                  row, vec, vec],
        out_specs=[row, row],
        out_shape=[jax.ShapeDtypeStruct((m, n), F32), jax.ShapeDtypeStruct((m, n), BF16)],
        compiler_params=_cparams(("parallel",)),
        name="out_proj_ln",
    )(x, w, res, g, b)


def _ffn_kernel(x_ref, wg_ref, wu_ref, wo_ref, res_ref, g_ref, b_ref, o_ref, ob_ref, acc_ref):
    j = pl.program_id(1)

    @pl.when(j == 0)
    def _():
        acc_ref[...] = jnp.zeros(acc_ref.shape, F32)

    x = x_ref[...]
    gate = _dot(x, wg_ref[...])
    up = _dot(x, wu_ref[...])
    f = (gate * jax.nn.sigmoid(gate) * up).astype(BF16)
    acc_ref[...] += _dot(f, wo_ref[...])

    @pl.when(j == pl.num_programs(1) - 1)
    def _():
        y = _ln_math(ALPHA * res_ref[...] + acc_ref[...], g_ref[...], b_ref[...])
        o_ref[...] = y
        ob_ref[...] = y.astype(BF16)


def ffn(x, w_in, w_out, res, g, b):
    m, k = x.shape
    n = w_out.shape[1]
    bm = _pick(m, 512)
    bn = 512
    nj = D_FF // bn
    row = pl.BlockSpec((bm, n), lambda i, j: (i, 0))
    vec = pl.BlockSpec((1, n), lambda i, j: (0, 0))
    return pl.pallas_call(
        _ffn_kernel,
        grid=(m // bm, nj),
        in_specs=[pl.BlockSpec((bm, k), lambda i, j: (i, 0)),
                  pl.BlockSpec((k, bn), lambda i, j: (0, j)),
                  pl.BlockSpec((k, bn), lambda i, j: (0, nj + j)),
                  pl.BlockSpec((bn, n), lambda i, j: (j, 0)),
                  row, vec, vec],
        out_specs=[row, row],
        out_shape=[jax.ShapeDtypeStruct((m, n), F32), jax.ShapeDtypeStruct((m, n), BF16)],
        scratch_shapes=[pltpu.VMEM((bm, n), F32)],
        compiler_params=_cparams(("parallel", "arbitrary")),
        name="ffn",
    )(x, w_in, w_in, w_out, res, g, b)


def _prep_params(w_in, q_norm_g, w_uq, kv_norm_g, w_ukv, rwkv_mu, rwkv_w0, rwkv_w_up, rwkv_a0,
                 rwkv_a_up, rwkv_g_up, rwkv_k_k, rwkv_k_a, rwkv_r_k, rwkv_lnx_g, rwkv_lnx_b,
                 conv_w, w_branch, w_out, ln1_g, ln1_b, w_ffn_in, w_ffn_out, ln2_g, ln2_b):
    nl = w_in.shape[0]
    z = lambda *s: jnp.zeros(s, F32)
    o = MLA_COLS
    kr = w_in[:, :, o - ROPE:o]
    kr_rot = jnp.concatenate([-kr[..., ROPE // 2:], kr[..., :ROPE // 2]], axis=-1)
    w_mla = jnp.concatenate([w_in[:, :, :o], z(nl, D_MODEL, 64), kr_rot, z(nl, D_MODEL, 64)], axis=-1)
    w_rw = jnp.concatenate([w_in[:, :, o:o + RWKV_COLS], z(nl, D_MODEL, RWKV_PAD - RWKV_COLS)], axis=-1)
    o += RWKV_COLS
    w_cv = w_in[:, :, o:o + CONV_COLS]
    o += CONV_COLS
    w_gt = w_in[:, :, o:]

    wq = w_uq.reshape(nl, Q_LORA, MLA_HEADS, NOPE + ROPE)
    x1 = wq[..., NOPE:NOPE + ROPE // 2]
    x2 = wq[..., NOPE + ROPE // 2:]
    pad = z(nl, Q_LORA, MLA_HEADS, 128 - ROPE)
    wq_all = jnp.concatenate([wq[..., :NOPE].reshape(nl, Q_LORA, WIDTH),
                              jnp.concatenate([x1, x2, pad], -1).reshape(nl, Q_LORA, WIDTH),
                              jnp.concatenate([-x2, x1, pad], -1).reshape(nl, Q_LORA, WIDTH)], axis=-1)
    wkv = w_ukv.reshape(nl, KV_LORA, MLA_HEADS, NOPE + VDIM)
    wkv_all = jnp.concatenate([wkv[..., :NOPE].reshape(nl, KV_LORA, WIDTH),
                               wkv[..., NOPE:].reshape(nl, KV_LORA, WIDTH)], axis=-1)

    zl = z(nl, DECAY_LORA, WIDTH)
    wup = jnp.concatenate([jnp.concatenate([rwkv_w_up[:, 0], zl], -1),
                           jnp.concatenate([zl, rwkv_w_up[:, 1]], -1)], axis=1)
    aup = jnp.concatenate([jnp.concatenate([rwkv_a_up[:, 0], zl], -1),
                           jnp.concatenate([zl, rwkv_a_up[:, 1]], -1)], axis=1)
    gup = jnp.concatenate([rwkv_g_up, z(nl, GD_PAD - GATE_LORA, WIDTH)], axis=1)
    mu = jnp.concatenate([rwkv_mu, z(nl, 2, RWKV_PAD - RWKV_COLS)], axis=-1)
    head = jnp.arange(SEG_LANES) // RW_N
    ones_bd = (head[:, None] == head[None, :]).astype(BF16)
    row = lambda t: t.reshape(nl, 1, -1)
    return dict(
        w_mla=w_mla.astype(BF16), w_rw=w_rw.astype(BF16), w_cv=w_cv.astype(BF16), w_gt=w_gt.astype(BF16),
        qg=row(q_norm_g), kvg=row(kv_norm_g), wq=wq_all.astype(BF16), wkv=wkv_all.astype(BF16),
        mu=mu, w0=row(rwkv_w0), a0=row(rwkv_a0), wup=wup.astype(BF16), aup=aup.astype(BF16),
        gup=gup.astype(BF16), k_k=row(rwkv_k_k), k_a=row(rwkv_k_a), r_k=row(rwkv_r_k),
        lnx_g=row(rwkv_lnx_g), lnx_b=row(rwkv_lnx_b), conv_w=conv_w,
        w_branch=w_branch.astype(BF16), w_out=w_out.astype(BF16),
        ln1_g=row(ln1_g), ln1_b=row(ln1_b), w_ffn_in=w_ffn_in.astype(BF16),
        w_ffn_out=w_ffn_out.astype(BF16), ln2_g=row(ln2_g), ln2_b=row(ln2_b),
    ), ones_bd


def _rope_tables(seq):
    pos = jnp.arange(seq, dtype=F32)
    inv = ROPE_THETA ** (-jnp.arange(0, ROPE, 2, dtype=F32) / ROPE)
    ang = pos[:, None] * inv[None, :]
    c, s = jnp.cos(ang), jnp.sin(ang)
    zpad = jnp.zeros((seq, 128 - ROPE), F32)
    return jnp.concatenate([c, c, zpad], -1), jnp.concatenate([s, s, zpad], -1)


def _layer(x, xb, p, ones_bd, cosz, sinz, bsz, seq):
    u_a = matmul(xb, p["w_mla"], MLA_PAD, name="in_mla")
    u_b = matmul(xb, p["w_rw"], RWKV_PAD // 2, name="in_rwkv")
    u_c = matmul(xb, p["w_cv"], 1024, out_dtype=BF16, name="in_conv")
    gates = matmul(xb, p["w_gt"], 1024, out_dtype=BF16, name="in_gate", sigmoid=True)

    q, k, v = mla_prep(u_a, p["qg"], p["kvg"], p["wq"], p["wkv"], cosz, sinz, seq)
    y_a = attention(q, k, v, bsz, seq)

    pp = dict(p, ones_bd=ones_bd)
    r, vv, a, lw0, lw1, k0, k1, b0, b1, bonus, g = rwkv_prep(u_b, pp, bsz, seq)
    y0, y1 = rwkv_scan(r, vv, a, lw0, k0, b0, lw1, k1, b1, bsz, seq)
    y_b = rwkv_post(y0, y1, bonus, g, p["lnx_g"], p["lnx_b"], ones_bd)

    y_c = conv_branch(u_c, p["conv_w"], bsz, seq)

    merged = merge_branches(y_a, y_b, y_c, gates, p["w_branch"])
    h, hb = out_proj_ln(merged, p["w_out"], x, p["ln1_g"], p["ln1_b"])
    return ffn(hb, p["w_ffn_in"], p["w_ffn_out"], h, p["ln2_g"], p["ln2_b"])


def _trunk(x3, ln_g, ln_b, params, ones_bd):
    bsz, seq, d = x3.shape
    cosz, sinz = _rope_tables(seq)
    x, xb = layer_norm_in(x3.reshape(bsz * seq, d), ln_g.reshape(1, d), ln_b.reshape(1, d))
    for l in range(DEPTH):
        p = {name: t[l] for name, t in params.items()}
        x, xb = _layer(x, xb, p, ones_bd, cosz, sinz, bsz, seq)
    return x.reshape(bsz, seq, d)


def kernel(x_prompt, x_sample, ln_in_g, ln_in_b, w_in, q_norm_g, w_uq, kv_norm_g, w_ukv, rwkv_mu, rwkv_w0, rwkv_w_up, rwkv_a0, rwkv_a_up, rwkv_g_up, rwkv_k_k, rwkv_k_a, rwkv_r_k, rwkv_lnx_g, rwkv_lnx_b, conv_w, w_branch, w_out, ln1_g, ln1_b, w_ffn_in, w_ffn_out, ln2_g, ln2_b):
    params, ones_bd = _prep_params(w_in, q_norm_g, w_uq, kv_norm_g, w_ukv, rwkv_mu, rwkv_w0, rwkv_w_up,
                                   rwkv_a0, rwkv_a_up, rwkv_g_up, rwkv_k_k, rwkv_k_a, rwkv_r_k,
                                   rwkv_lnx_g, rwkv_lnx_b, conv_w, w_branch, w_out, ln1_g, ln1_b,
                                   w_ffn_in, w_ffn_out, ln2_g, ln2_b)
    y_prompt = _trunk(x_prompt, ln_in_g, ln_in_b, params, ones_bd)
    y_sample = _trunk(x_sample, ln_in_g, ln_in_b, params, ones_bd)
    return (y_prompt, y_sample)
```

```python
import functools

import jax
import jax.numpy as jnp
from jax import lax
from jax.experimental import pallas as pl
from jax.experimental.pallas import tpu as pltpu

F32 = jnp.float32
BF16 = jnp.bfloat16

D_MODEL = 2048
DEPTH = 4
WIDTH = D_MODEL // 2
MLA_HEADS = 8
NOPE = 128
ROPE = 64
VDIM = 128
Q_LORA = 768
KV_LORA = 512
ROPE_THETA = 10000.0
RW_HEADS = 16
RW_N = 64
DECAY_LORA = 64
AAA_LORA = 64
GATE_LORA = 160
D_FF = 5632
LN_EPS = 1e-5
RMS_EPS = 1e-6
LNX_EPS = 1e-5 * RW_N
ALPHA = (2 * DEPTH) ** 0.25
LOG2_E = 1.4426950408889634
EXP_NEG_HALF = 0.6065306597126334
MLA_COLS = Q_LORA + KV_LORA + ROPE
RWKV_COLS = 3 * WIDTH + 2 * DECAY_LORA + 2 * AAA_LORA + GATE_LORA
CONV_COLS = 3 * WIDTH
GATE_COLS = 3 * D_MODEL
MLA_PAD = 1536
RWKV_PAD = 3584
GD_PAD = 256
QK_PAD = 256
CHUNK = 64
CONV_HALO = 16
SEG_LANES = 256

VMEM_LIMIT = 56 * 1024 * 1024


def _cparams(sem):
    return pltpu.CompilerParams(dimension_semantics=sem, vmem_limit_bytes=VMEM_LIMIT)


def _dot(a, b):
    return jnp.dot(a, b, preferred_element_type=F32)


def _dot_nt(a, b):
    return lax.dot_general(a, b, (((1,), (1,)), ((), ())), preferred_element_type=F32)


def _dot_tn(a, b):
    return lax.dot_general(a, b, (((0,), (0,)), ((), ())), preferred_element_type=F32)


def _split2(x):
    hi = x.astype(BF16)
    lo = (x - hi.astype(F32)).astype(BF16)
    return hi, lo


def _split3(x):
    hi = x.astype(BF16)
    r1 = x - hi.astype(F32)
    mid = r1.astype(BF16)
    lo = (r1 - mid.astype(F32)).astype(BF16)
    return hi, mid, lo


def _seg_sum(x, ones_bd):
    hi, lo = _split2(x)
    parts = []
    for c in range(0, x.shape[1], SEG_LANES):
        parts.append(_dot(hi[:, c:c + SEG_LANES], ones_bd) + _dot(lo[:, c:c + SEG_LANES], ones_bd))
    return jnp.concatenate(parts, axis=1)


def _pick(n, target):
    b = min(n, target)
    while n % b:
        b //= 2
    return b


def _ln_math(x, g, b):
    mu = jnp.mean(x, axis=-1, keepdims=True)
    xc = x - mu
    var = jnp.mean(xc * xc, axis=-1, keepdims=True)
    return xc * lax.rsqrt(var + LN_EPS) * g + b


def _ln_kernel(x_ref, g_ref, b_ref, o_ref, ob_ref):
    y = _ln_math(x_ref[...], g_ref[...], b_ref[...])
    o_ref[...] = y
    ob_ref[...] = y.astype(BF16)


def layer_norm_in(x, g, b):
    m, d = x.shape
    bm = _pick(m, 512)
    return pl.pallas_call(
        _ln_kernel,
        grid=(m // bm,),
        in_specs=[pl.BlockSpec((bm, d), lambda i: (i, 0)),
                  pl.BlockSpec((1, d), lambda i: (0, 0)),
                  pl.BlockSpec((1, d), lambda i: (0, 0))],
        out_specs=[pl.BlockSpec((bm, d), lambda i: (i, 0)),
                   pl.BlockSpec((bm, d), lambda i: (i, 0))],
        out_shape=[jax.ShapeDtypeStruct((m, d), F32), jax.ShapeDtypeStruct((m, d), BF16)],
        compiler_params=_cparams(("parallel",)),
        name="ln_in",
    )(x, g, b)


def _mm_kernel(x_ref, w_ref, o_ref):
    o_ref[...] = _dot(x_ref[...], w_ref[...]).astype(o_ref.dtype)


def matmul(x, w, bn, out_dtype=F32, name="mm"):
    m, k = x.shape
    n = w.shape[1]
    bm = _pick(m, 1024)
    return pl.pallas_call(
        _mm_kernel,
        grid=(m // bm, n // bn),
        in_specs=[pl.BlockSpec((bm, k), lambda i, j: (i, 0)),
                  pl.BlockSpec((k, bn), lambda i, j: (0, j))],
        out_specs=pl.BlockSpec((bm, bn), lambda i, j: (i, j)),
        out_shape=jax.ShapeDtypeStruct((m, n), out_dtype),
        compiler_params=_cparams(("parallel", "arbitrary")),
        name=name,
    )(x, w)


def _rms(x, g):
    return x * lax.rsqrt(jnp.mean(x * x, axis=-1, keepdims=True) + RMS_EPS) * g


def _mla_prep_kernel(u_ref, qg_ref, kvg_ref, wq_ref, wkv_ref, cos_ref, sin_ref,
                     q_ref, k_ref, v_ref):
    scale = (NOPE + ROPE) ** -0.5 * LOG2_E
    cz = cos_ref[...]
    sz = sin_ref[...]
    cq = _rms(u_ref[:, :Q_LORA], qg_ref[...]).astype(BF16)
    pq = _dot(cq, wq_ref[...])
    ckv = _rms(u_ref[:, Q_LORA:Q_LORA + KV_LORA], kvg_ref[...]).astype(BF16)
    pkv = _dot(ckv, wkv_ref[...])
    o = Q_LORA + KV_LORA
    krz = (u_ref[:, o:o + 128] * cz + u_ref[:, o + 128:o + 256] * sz).astype(BF16)
    lane = lax.broadcasted_iota(jnp.int32, (cz.shape[0], 128), 1)
    one_col = jnp.where(lane == 0, 1.0, 0.0).astype(BF16)
    for h in range(MLA_HEADS):
        a, b = h * 128, (h + 1) * 128
        q_ref[:, h * QK_PAD:h * QK_PAD + 128] = (pq[:, a:b] * scale).astype(BF16)
        q_ref[:, h * QK_PAD + 128:(h + 1) * QK_PAD] = (
            (pq[:, WIDTH + a:WIDTH + b] * cz + pq[:, 2 * WIDTH + a:2 * WIDTH + b] * sz) * scale
        ).astype(BF16)
        k_ref[:, h * QK_PAD:h * QK_PAD + 128] = pkv[:, a:b].astype(BF16)
        k_ref[:, h * QK_PAD + 128:(h + 1) * QK_PAD] = krz
        v_ref[:, h * QK_PAD:h * QK_PAD + 128] = pkv[:, WIDTH + a:WIDTH + b].astype(BF16)
        v_ref[:, h * QK_PAD + 128:(h + 1) * QK_PAD] = one_col


def mla_prep(u_a, qg, kvg, wq, wkv, cosz, sinz, seq):
    m = u_a.shape[0]
    bm = _pick(seq, 512)
    nb = seq // bm
    full = lambda i: (0, 0)
    return pl.pallas_call(
        _mla_prep_kernel,
        grid=(m // bm,),
        in_specs=[pl.BlockSpec((bm, MLA_PAD), lambda i: (i, 0)),
                  pl.BlockSpec((1, Q_LORA), full),
                  pl.BlockSpec((1, KV_LORA), full),
                  pl.BlockSpec(wq.shape, full),
                  pl.BlockSpec(wkv.shape, full),
                  pl.BlockSpec((bm, 128), lambda i: (i % nb, 0)),
                  pl.BlockSpec((bm, 128), lambda i: (i % nb, 0))],
        out_specs=[pl.BlockSpec((bm, MLA_HEADS * QK_PAD), lambda i: (i, 0)),
                   pl.BlockSpec((bm, MLA_HEADS * QK_PAD), lambda i: (i, 0)),
                   pl.BlockSpec((bm, MLA_HEADS * QK_PAD), lambda i: (i, 0))],
        out_shape=[jax.ShapeDtypeStruct((m, MLA_HEADS * QK_PAD), BF16)] * 3,
        compiler_params=_cparams(("parallel",)),
        name="mla_prep",
    )(u_a, qg, kvg, wq, wkv, cosz, sinz)


def _attn_kernel(q_ref, k_ref, v_ref, o_ref, *, nkv, bk):
    q = q_ref[...]
    m = jnp.full((q.shape[0], 1), -jnp.inf, F32)
    acc = jnp.zeros((q.shape[0], QK_PAD), F32)
    s = _dot_nt(q, k_ref[0:bk, :])
    for j in range(nkv):
        if j + 1 < nkv:
            s_next = _dot_nt(q, k_ref[(j + 1) * bk:(j + 2) * bk, :])
        m_new = jnp.maximum(m, jnp.max(s, axis=-1, keepdims=True))
        alpha = jnp.exp2(m - m_new)
        p = jnp.exp2((s - m_new).astype(BF16))
        acc = alpha * acc + _dot(p, v_ref[j * bk:(j + 1) * bk, :])
        m = m_new
        if j + 1 < nkv:
            s = s_next
    o_ref[...] = (acc[:, :VDIM] / acc[:, VDIM:VDIM + 1]).astype(o_ref.dtype)


def attention(q, k, v, bsz, seq):
    m = q.shape[0]
    bq = _pick(seq, 1024)
    bk = _pick(seq, 1024)
    nq = seq // bq
    return pl.pallas_call(
        functools.partial(_attn_kernel, nkv=seq // bk, bk=bk),
        grid=(bsz, MLA_HEADS, nq),
        in_specs=[pl.BlockSpec((bq, QK_PAD), lambda b, h, i: (b * nq + i, h)),
                  pl.BlockSpec((seq, QK_PAD), lambda b, h, i: (b, h)),
                  pl.BlockSpec((seq, QK_PAD), lambda b, h, i: (b, h))],
        out_specs=pl.BlockSpec((bq, VDIM), lambda b, h, i: (b * nq + i, h)),
        out_shape=jax.ShapeDtypeStruct((m, WIDTH), BF16),
        compiler_params=_cparams(("parallel", "parallel", "arbitrary")),
        name="mla_attn",
    )(q, k, v)


def _shifted(u, prev_row, next_row):
    n = u.shape[0]
    row = lax.broadcasted_iota(jnp.int32, (n, 1), 0)
    up = jnp.where(row == 0, prev_row, pltpu.roll(u, 1, axis=0))
    un = jnp.where(row == n - 1, next_row, pltpu.roll(u, n - 1, axis=0))
    return up, un


def _halo_specs(bm, width, nb, bsz, hr):
    per = bm // hr
    last = bsz * nb * per - 1
    cur = pl.BlockSpec((bm, width), lambda b, i: (b * nb + i, 0))
    prev = pl.BlockSpec((hr, width), lambda b, i: (jnp.maximum((b * nb + i) * per - 1, 0), 0))
    nxt = pl.BlockSpec((hr, width), lambda b, i: (jnp.minimum((b * nb + i + 1) * per, last), 0))
    return cur, prev, nxt


def _rwkv_prep_kernel(u_ref, up_ref, un_ref, mu_ref, w0_ref, a0_ref, wup_ref, aup_ref, gup_ref,
                      kk_ref, ka_ref, rk_ref, ones_ref,
                      r_o, v_o, a_o, lw0_o, lw1_o, k0_o, k1_o, b0_o, b1_o, bonus_o, g_o):
    i = pl.program_id(1)
    u = u_ref[...]
    prev_row = jnp.where(i == 0, 0.0, up_ref[7:8, :])
    next_row = jnp.where(i == pl.num_programs(1) - 1, 0.0, un_ref[0:1, :])
    up, un = _shifted(u, prev_row, next_row)
    us = u + mu_ref[0:1, :] * (up - u) + mu_ref[1:2, :] * (un - u)
    r = us[:, 0:WIDTH]
    k = us[:, WIDTH:2 * WIDTH]
    v = us[:, 2 * WIDTH:3 * WIDTH]
    o = 3 * WIDTH
    wd = us[:, o:o + 2 * DECAY_LORA]
    ad = us[:, o + 128:o + 128 + 2 * AAA_LORA]
    gd = us[:, o + 256:o + 256 + GD_PAD]
    g_o[...] = _dot(jax.nn.sigmoid(gd).astype(BF16), gup_ref[...]).astype(g_o.dtype)
    t = w0_ref[...] + _dot(jnp.tanh(wd).astype(BF16), wup_ref[...])
    lw = -EXP_NEG_HALF * jax.nn.sigmoid(t)
    al = jax.nn.sigmoid(a0_ref[...] + _dot(ad.astype(BF16), aup_ref[...]))
    ones_bd = ones_ref[...]
    kkr = k * kk_ref[...]
    kk = kkr * lax.rsqrt(jnp.maximum(_seg_sum(kkr * kkr, ones_bd), 1e-24))
    ka = ka_ref[...]
    a_0 = al[:, :WIDTH]
    a_1 = al[:, WIDTH:]
    k_0 = k * (1.0 + (a_0 - 1.0) * ka)
    k_1 = k * (1.0 + (a_1 - 1.0) * ka)
    r_o[...] = r.astype(r_o.dtype)
    v_o[...] = v.astype(v_o.dtype)
    a_o[...] = (-kk).astype(a_o.dtype)
    lw0_o[...] = lw[:, :WIDTH]
    lw1_o[...] = lw[:, WIDTH:]
    k0_o[...] = k_0.astype(k0_o.dtype)
    k1_o[...] = k_1.astype(k1_o.dtype)
    b0_o[...] = (kk * a_0).astype(b0_o.dtype)
    b1_o[...] = (kk * a_1).astype(b1_o.dtype)
    bonus_o[...] = (_seg_sum(r * rk_ref[...] * (k_0 + k_1), ones_bd) * v).astype(bonus_o.dtype)


def rwkv_prep(u_b, p, bsz, seq):
    m = u_b.shape[0]
    bm = _pick(seq, 128)
    nb = seq // bm
    cur, prev, nxt = _halo_specs(bm, RWKV_PAD, nb, bsz, 8)
    full = lambda b, i: (0, 0)
    consts = [p["mu"], p["w0"], p["a0"], p["wup"], p["aup"], p["gup"], p["k_k"], p["k_a"], p["r_k"],
              p["ones_bd"]]
    out_spec = pl.BlockSpec((bm, WIDTH), lambda b, i: (b * nb + i, 0))
    return pl.pallas_call(
        _rwkv_prep_kernel,
        grid=(bsz, nb),
        in_specs=[cur, prev, nxt] + [pl.BlockSpec(c.shape, full) for c in consts],
        out_specs=[out_spec] * 11,
        out_shape=[jax.ShapeDtypeStruct((m, WIDTH), dt)
                   for dt in (BF16, BF16, BF16, F32, F32, BF16, BF16, BF16, BF16, BF16, BF16)],
        compiler_params=_cparams(("parallel", "arbitrary")),
        name="rwkv_prep",
    )(u_b, u_b, u_b, *consts)


def _scan_chunk_operands(refs, rows, reverse):
    r_ref, lw_ref, k_ref, v_ref, a_ref, b_ref = refs
    ti = lax.broadcasted_iota(jnp.int32, (CHUNK, CHUNK), 0)
    si = lax.broadcasted_iota(jnp.int32, (CHUNK, CHUNK), 1)
    tri = ((si >= ti) if reverse else (si <= ti)).astype(BF16)
    last = 0 if reverse else CHUNK - 1
    lw = lw_ref[rows, :]
    l_hi, l_mid, l_lo = _split3(lw)
    cum = _dot(tri, l_hi) + _dot(tri, l_mid) + _dot(tri, l_lo)
    gam = jnp.exp(cum)
    ginv = jnp.exp(-cum)
    rt = r_ref[rows, :].astype(F32) * gam
    at = a_ref[rows, :].astype(F32) * jnp.exp(cum - lw)
    kt = k_ref[rows, :].astype(F32) * ginv
    bt = b_ref[rows, :].astype(F32) * ginv
    v = v_ref[rows, :].astype(F32)
    g_end = gam[last:last + 1, :]
    out = []
    for h in range(RW_HEADS):
        sl = slice(h * RW_N, (h + 1) * RW_N)
        ar = jnp.concatenate([at[:, sl].astype(BF16), rt[:, sl].astype(BF16)], axis=0)
        bk = jnp.concatenate([bt[:, sl].astype(BF16), kt[:, sl].astype(BF16)], axis=0)
        out.append((ar, bk, v[:, sl].astype(BF16), g_end[:, sl]))
    return out


def _scan_kernel(rf_ref, lwf_ref, kf_ref, vf_ref, af_ref, bf_ref,
                 rb_ref, lwb_ref, kb_ref, vb_ref, ab_ref, bb_ref,
                 yf_ref, yb_ref, s_ref, *, nchunk):
    @pl.when(pl.program_id(1) == 0)
    def _():
        s_ref[...] = jnp.zeros(s_ref.shape, F32)

    ti = lax.broadcasted_iota(jnp.int32, (CHUNK, 2 * CHUNK), 0)
    lane = lax.broadcasted_iota(jnp.int32, (CHUNK, 2 * CHUNK), 1)
    si = lane & (CHUNK - 1)
    left = lane < CHUNK
    masks = []
    for reverse in (False, True):
        strict = (si > ti) if reverse else (si < ti)
        incl = (si >= ti) if reverse else (si <= ti)
        masks += [(strict, incl)] * RW_HEADS
    n = 2 * RW_HEADS
    streams = range(n)

    def chunk(ci, carry):
        rows_f = pl.ds(pl.multiple_of(ci * CHUNK, CHUNK), CHUNK)
        rows_b = pl.ds(pl.multiple_of((nchunk - 1 - ci) * CHUNK, CHUNK), CHUNK)
        ops = (_scan_chunk_operands((rf_ref, lwf_ref, kf_ref, vf_ref, af_ref, bf_ref), rows_f, False)
               + _scan_chunk_operands((rb_ref, lwb_ref, kb_ref, vb_ref, ab_ref, bb_ref), rows_b, True))
        ar = [o[0] for o in ops]
        bk = [o[1] for o in ops]
        v_h = [o[2] for o in ops]
        g_end = [o[3] for o in ops]
        s0 = [s_ref[i] for i in streams]
        g = [_dot_nt(ar[i], bk[i]) for i in streams]
        hs = [_dot_nt(ar[i], s0[i].astype(BF16)) for i in streams]
        g_top = [jnp.where(masks[i][0], g[i][:CHUNK], 0.0) for i in streams]
        g_bot = [jnp.where(masks[i][1], g[i][CHUNK:], 0.0).astype(BF16) for i in streams]
        w_in = [hs[i][:CHUNK] + _dot(g_top[i][:, CHUNK:].astype(BF16), v_h[i]) for i in streams]
        zero = jnp.zeros((CHUNK, CHUNK), F32)
        z = [jnp.where(left, g_top[i], jnp.concatenate([zero, w_in[i]], axis=1)) for i in streams]
        for _ in range(CHUNK.bit_length() - 1):
            r = [_dot(z[i][:, :CHUNK].astype(BF16), z[i].astype(BF16)) for i in streams]
            z = [r[i] + jnp.where(left, 0.0, z[i]) for i in streams]
        uv = [jnp.concatenate([z[i][:, CHUNK:].astype(BF16), v_h[i]], axis=0) for i in streams]
        y = [hs[i][CHUNK:] + _dot(g_bot[i], uv[i]) for i in streams]
        ds = [_dot_tn(uv[i], bk[i]) for i in streams]
        for i in streams:
            s_ref[i] = (s0[i] + ds[i]) * g_end[i]
            h = i % RW_HEADS
            sl = slice(h * RW_N, (h + 1) * RW_N)
            if i < RW_HEADS:
                yf_ref[rows_f, sl] = y[i]
            else:
                yb_ref[rows_b, sl] = y[i]
        return carry

    lax.fori_loop(0, nchunk, chunk, 0)


def rwkv_scan(r, v, a, lw0, k0, b0, lw1, k1, b1, bsz, seq):
    m = r.shape[0]
    bt = _pick(seq, 256)
    nb = seq // bt
    fwd = pl.BlockSpec((bt, WIDTH), lambda bi, i: (bi * nb + i, 0))
    bwd = pl.BlockSpec((bt, WIDTH), lambda bi, i: (bi * nb + nb - 1 - i, 0))
    return pl.pallas_call(
        functools.partial(_scan_kernel, nchunk=bt // CHUNK),
        grid=(bsz, nb),
        in_specs=[fwd] * 6 + [bwd] * 6,
        out_specs=[fwd, bwd],
        out_shape=[jax.ShapeDtypeStruct((m, WIDTH), F32)] * 2,
        scratch_shapes=[pltpu.VMEM((2 * RW_HEADS, RW_N, RW_N), F32)],
        compiler_params=_cparams(("parallel", "arbitrary")),
        name="rwkv_scan",
    )(r, lw0, k0, v, a, b0, r, lw1, k1, v, a, b1)


def _rwkv_post_kernel(y0_ref, y1_ref, bonus_ref, g_ref, lg_ref, lb_ref, ones_ref, o_ref):
    ones_bd = ones_ref[...]
    y = y0_ref[...] + y1_ref[...]
    mu = _seg_sum(y, ones_bd) * (1.0 / RW_N)
    yc = y - mu
    var = _seg_sum(yc * yc, ones_bd) * (1.0 / RW_N)
    yn = yc * lax.rsqrt(var + LNX_EPS) * lg_ref[...] + lb_ref[...]
    o_ref[...] = ((yn + bonus_ref[...]) * g_ref[...]).astype(o_ref.dtype)


def rwkv_post(y0, y1, bonus, g, lg, lb, ones_bd):
    m = y0.shape[0]
    bm = _pick(m, 512)
    spec = pl.BlockSpec((bm, WIDTH), lambda i: (i, 0))
    full = lambda i: (0, 0)
    return pl.pallas_call(
        _rwkv_post_kernel,
        grid=(m // bm,),
        in_specs=[spec] * 4 + [pl.BlockSpec((1, WIDTH), full), pl.BlockSpec((1, WIDTH), full),
                               pl.BlockSpec((SEG_LANES, SEG_LANES), full)],
        out_specs=spec,
        out_shape=jax.ShapeDtypeStruct((m, WIDTH), BF16),
        compiler_params=_cparams(("parallel",)),
        name="rwkv_post",
    )(y0, y1, bonus, g, lg, lb, ones_bd)


def _conv_kernel(u_ref, up_ref, un_ref, w_ref, o_ref):
    i = pl.program_id(1)
    u = u_ref[...].astype(F32)
    hp = u[:, WIDTH:2 * WIDTH] * u[:, 2 * WIDTH:]
    pr = up_ref[...].astype(F32)[CONV_HALO - 1:CONV_HALO, :]
    nr = un_ref[...].astype(F32)[0:1, :]
    prev_row = jnp.where(i == 0, 0.0, pr[:, WIDTH:2 * WIDTH] * pr[:, 2 * WIDTH:])
    next_row = jnp.where(i == pl.num_programs(1) - 1, 0.0, nr[:, WIDTH:2 * WIDTH] * nr[:, 2 * WIDTH:])
    hm, hn = _shifted(hp, prev_row, next_row)
    conv = w_ref[0:1, :] * hm + w_ref[1:2, :] * hp + w_ref[2:3, :] * hn
    o_ref[...] = (u[:, :WIDTH] * conv).astype(o_ref.dtype)


def conv_branch(u_c, conv_w, bsz, seq):
    m = u_c.shape[0]
    bm = _pick(seq, 512)
    nb = seq // bm
    cur, prev, nxt = _halo_specs(bm, CONV_COLS, nb, bsz, CONV_HALO)
    return pl.pallas_call(
        _conv_kernel,
        grid=(bsz, nb),
        in_specs=[cur, prev, nxt, pl.BlockSpec((3, WIDTH), lambda b, i: (0, 0))],
        out_specs=pl.BlockSpec((bm, WIDTH), lambda b, i: (b * nb + i, 0)),
        out_shape=jax.ShapeDtypeStruct((m, WIDTH), BF16),
        compiler_params=_cparams(("parallel", "arbitrary")),
        name="conv_branch",
    )(u_c, u_c, u_c, conv_w)


def _merge_kernel(ya_ref, yb_ref, yc_ref, g0_ref, g1_ref, g2_ref, w_ref, o_ref):
    acc = jax.nn.sigmoid(g0_ref[...].astype(F32)) * _dot(ya_ref[...], w_ref[0])
    acc = acc + jax.nn.sigmoid(g1_ref[...].astype(F32)) * _dot(yb_ref[...], w_ref[1])
    acc = acc + jax.nn.sigmoid(g2_ref[...].astype(F32)) * _dot(yc_ref[...], w_ref[2])
    o_ref[...] = acc.astype(o_ref.dtype)


def merge_branches(ya, yb, yc, gates, w_branch):
    m = ya.shape[0]
    bm = _pick(m, 1024)
    bn = 1024
    nj = D_MODEL // bn
    yspec = pl.BlockSpec((bm, WIDTH), lambda j, i: (i, 0))
    gspecs = [pl.BlockSpec((bm, bn), functools.partial(lambda j, i, t: (i, t * nj + j), t=t))
              for t in range(3)]
    return pl.pallas_call(
        _merge_kernel,
        grid=(nj, m // bm),
        in_specs=[yspec] * 3 + gspecs + [pl.BlockSpec((3, WIDTH, bn), lambda j, i: (0, 0, j))],
        out_specs=pl.BlockSpec((bm, bn), lambda j, i: (i, j)),
        out_shape=jax.ShapeDtypeStruct((m, D_MODEL), BF16),
        compiler_params=_cparams(("arbitrary", "arbitrary")),
        name="merge",
    )(ya, yb, yc, gates, gates, gates, w_branch)


def _out_proj_ln_kernel(x_ref, w_ref, res_ref, g_ref, b_ref, o_ref, ob_ref):
    y = _ln_math(ALPHA * res_ref[...] + _dot(x_ref[...], w_ref[...]), g_ref[...], b_ref[...])
    o_ref[...] = y
    ob_ref[...] = y.astype(BF16)


def out_proj_ln(x, w, res, g, b):
    m, k = x.shape
    n = w.shape[1]
    bm = _pick(m, 512)
    row = pl.BlockSpec((bm, n), lambda i: (i, 0))
    vec = pl.BlockSpec((1, n), lambda i: (0, 0))
    return pl.pallas_call(
        _out_proj_ln_kernel,
        grid=(m // bm,),
        in_specs=[pl.BlockSpec((bm, k), lambda i: (i, 0)), pl.BlockSpec((k, n), lambda i: (0, 0)),
                  row, vec, vec],
        out_specs=[row, row],
        out_shape=[jax.ShapeDtypeStruct((m, n), F32), jax.ShapeDtypeStruct((m, n), BF16)],
        compiler_params=_cparams(("parallel",)),
        name="out_proj_ln",
    )(x, w, res, g, b)


def _ffn_kernel(x_ref, wg_ref, wu_ref, wo_ref, res_ref, g_ref, b_ref, o_ref, ob_ref, acc_ref):
    j = pl.program_id(1)

    @pl.when(j == 0)
    def _():
        acc_ref[...] = jnp.zeros(acc_ref.shape, F32)

    x = x_ref[...]
    gate = _dot(x, wg_ref[...])
    up = _dot(x, wu_ref[...])
    f = (gate * jax.nn.sigmoid(gate) * up).astype(BF16)
    acc_ref[...] += _dot(f, wo_ref[...])

    @pl.when(j == pl.num_programs(1) - 1)
    def _():
        y = _ln_math(ALPHA * res_ref[...] + acc_ref[...], g_ref[...], b_ref[...])
        o_ref[...] = y
        ob_ref[...] = y.astype(BF16)


def ffn(x, w_in, w_out, res, g, b):
    m, k = x.shape
    n = w_out.shape[1]
    bm = _pick(m, 512)
    bn = 512
    nj = D_FF // bn
    row = pl.BlockSpec((bm, n), lambda i, j: (i, 0))
    vec = pl.BlockSpec((1, n), lambda i, j: (0, 0))
    return pl.pallas_call(
        _ffn_kernel,
        grid=(m // bm, nj),
        in_specs=[pl.BlockSpec((bm, k), lambda i, j: (i, 0)),
                  pl.BlockSpec((k, bn), lambda i, j: (0, j)),
                  pl.BlockSpec((k, bn), lambda i, j: (0, nj + j)),
                  pl.BlockSpec((bn, n), lambda i, j: (j, 0)),
                  row, vec, vec],
        out_specs=[row, row],
        out_shape=[jax.ShapeDtypeStruct((m, n), F32), jax.ShapeDtypeStruct((m, n), BF16)],
        scratch_shapes=[pltpu.VMEM((bm, n), F32)],
        compiler_params=_cparams(("parallel", "arbitrary")),
        name="ffn",
    )(x, w_in, w_in, w_out, res, g, b)


def _prep_params(w_in, q_norm_g, w_uq, kv_norm_g, w_ukv, rwkv_mu, rwkv_w0, rwkv_w_up, rwkv_a0,
                 rwkv_a_up, rwkv_g_up, rwkv_k_k, rwkv_k_a, rwkv_r_k, rwkv_lnx_g, rwkv_lnx_b,
                 conv_w, w_branch, w_out, ln1_g, ln1_b, w_ffn_in, w_ffn_out, ln2_g, ln2_b):
    nl = w_in.shape[0]
    z = lambda *s: jnp.zeros(s, F32)
    o = MLA_COLS
    kr = w_in[:, :, o - ROPE:o]
    kr_rot = jnp.concatenate([-kr[..., ROPE // 2:], kr[..., :ROPE // 2]], axis=-1)
    w_mla = jnp.concatenate([w_in[:, :, :o], z(nl, D_MODEL, 64), kr_rot, z(nl, D_MODEL, 64)], axis=-1)
    w_rw = jnp.concatenate([w_in[:, :, o:o + RWKV_COLS], z(nl, D_MODEL, RWKV_PAD - RWKV_COLS)], axis=-1)
    o += RWKV_COLS
    w_cv = w_in[:, :, o:o + CONV_COLS]
    o += CONV_COLS
    w_gt = w_in[:, :, o:]

    wq = w_uq.reshape(nl, Q_LORA, MLA_HEADS, NOPE + ROPE)
    x1 = wq[..., NOPE:NOPE + ROPE // 2]
    x2 = wq[..., NOPE + ROPE // 2:]
    pad = z(nl, Q_LORA, MLA_HEADS, 128 - ROPE)
    wq_all = jnp.concatenate([wq[..., :NOPE].reshape(nl, Q_LORA, WIDTH),
                              jnp.concatenate([x1, x2, pad], -1).reshape(nl, Q_LORA, WIDTH),
                              jnp.concatenate([-x2, x1, pad], -1).reshape(nl, Q_LORA, WIDTH)], axis=-1)
    wkv = w_ukv.reshape(nl, KV_LORA, MLA_HEADS, NOPE + VDIM)
    wkv_all = jnp.concatenate([wkv[..., :NOPE].reshape(nl, KV_LORA, WIDTH),
                               wkv[..., NOPE:].reshape(nl, KV_LORA, WIDTH)], axis=-1)

    zl = z(nl, DECAY_LORA, WIDTH)
    wup = jnp.concatenate([jnp.concatenate([rwkv_w_up[:, 0], zl], -1),
                           jnp.concatenate([zl, rwkv_w_up[:, 1]], -1)], axis=1)
    aup = jnp.concatenate([jnp.concatenate([rwkv_a_up[:, 0], zl], -1),
                           jnp.concatenate([zl, rwkv_a_up[:, 1]], -1)], axis=1)
    gup = jnp.concatenate([rwkv_g_up, z(nl, GD_PAD - GATE_LORA, WIDTH)], axis=1)
    mu = jnp.concatenate([rwkv_mu, z(nl, 2, RWKV_PAD - RWKV_COLS)], axis=-1)
    head = jnp.arange(SEG_LANES) // RW_N
    ones_bd = (head[:, None] == head[None, :]).astype(BF16)
    row = lambda t: t.reshape(nl, 1, -1)
    return dict(
        w_mla=w_mla.astype(BF16), w_rw=w_rw.astype(BF16), w_cv=w_cv.astype(BF16), w_gt=w_gt.astype(BF16),
        qg=row(q_norm_g), kvg=row(kv_norm_g), wq=wq_all.astype(BF16), wkv=wkv_all.astype(BF16),
        mu=mu, w0=row(rwkv_w0), a0=row(rwkv_a0), wup=wup.astype(BF16), aup=aup.astype(BF16),
        gup=gup.astype(BF16), k_k=row(rwkv_k_k), k_a=row(rwkv_k_a), r_k=row(rwkv_r_k),
        lnx_g=row(rwkv_lnx_g), lnx_b=row(rwkv_lnx_b), conv_w=conv_w,
        w_branch=w_branch.astype(BF16), w_out=w_out.astype(BF16),
        ln1_g=row(ln1_g), ln1_b=row(ln1_b), w_ffn_in=w_ffn_in.astype(BF16),
        w_ffn_out=w_ffn_out.astype(BF16), ln2_g=row(ln2_g), ln2_b=row(ln2_b),
    ), ones_bd


def _rope_tables(seq):
    pos = jnp.arange(seq, dtype=F32)
    inv = ROPE_THETA ** (-jnp.arange(0, ROPE, 2, dtype=F32) / ROPE)
    ang = pos[:, None] * inv[None, :]
    c, s = jnp.cos(ang), jnp.sin(ang)
    zpad = jnp.zeros((seq, 128 - ROPE), F32)
    return jnp.concatenate([c, c, zpad], -1), jnp.concatenate([s, s, zpad], -1)


def _layer(x, xb, p, ones_bd, cosz, sinz, bsz, seq):
    u_a = matmul(xb, p["w_mla"], MLA_PAD, name="in_mla")
    u_b = matmul(xb, p["w_rw"], RWKV_PAD // 2, name="in_rwkv")
    u_c = matmul(xb, p["w_cv"], 1024, out_dtype=BF16, name="in_conv")
    gates = matmul(xb, p["w_gt"], 1024, out_dtype=BF16, name="in_gate")

    q, k, v = mla_prep(u_a, p["qg"], p["kvg"], p["wq"], p["wkv"], cosz, sinz, seq)
    y_a = attention(q, k, v, bsz, seq)

    pp = dict(p, ones_bd=ones_bd)
    r, vv, a, lw0, lw1, k0, k1, b0, b1, bonus, g = rwkv_prep(u_b, pp, bsz, seq)
    y0, y1 = rwkv_scan(r, vv, a, lw0, k0, b0, lw1, k1, b1, bsz, seq)
    y_b = rwkv_post(y0, y1, bonus, g, p["lnx_g"], p["lnx_b"], ones_bd)

    y_c = conv_branch(u_c, p["conv_w"], bsz, seq)

    merged = merge_branches(y_a, y_b, y_c, gates, p["w_branch"])
    h, hb = out_proj_ln(merged, p["w_out"], x, p["ln1_g"], p["ln1_b"])
    return ffn(hb, p["w_ffn_in"], p["w_ffn_out"], h, p["ln2_g"], p["ln2_b"])


def _trunk(x3, ln_g, ln_b, params, ones_bd):
    bsz, seq, d = x3.shape
    cosz, sinz = _rope_tables(seq)
    x, xb = layer_norm_in(x3.reshape(bsz * seq, d), ln_g.reshape(1, d), ln_b.reshape(1, d))
    for l in range(DEPTH):
        p = {name: t[l] for name, t in params.items()}
        x, xb = _layer(x, xb, p, ones_bd, cosz, sinz, bsz, seq)
    return x.reshape(bsz, seq, d)


def kernel(x_prompt, x_sample, ln_in_g, ln_in_b, w_in, q_norm_g, w_uq, kv_norm_g, w_ukv, rwkv_mu, rwkv_w0, rwkv_w_up, rwkv_a0, rwkv_a_up, rwkv_g_up, rwkv_k_k, rwkv_k_a, rwkv_r_k, rwkv_lnx_g, rwkv_lnx_b, conv_w, w_branch, w_out, ln1_g, ln1_b, w_ffn_in, w_ffn_out, ln2_g, ln2_b):
    params, ones_bd = _prep_params(w_in, q_norm_g, w_uq, kv_norm_g, w_ukv, rwkv_mu, rwkv_w0, rwkv_w_up,
                                   rwkv_a0, rwkv_a_up, rwkv_g_up, rwkv_k_k, rwkv_k_a, rwkv_r_k,
                                   rwkv_lnx_g, rwkv_lnx_b, conv_w, w_branch, w_out, ln1_g, ln1_b,
                                   w_ffn_in, w_ffn_out, ln2_g, ln2_b)
    y_prompt = _trunk(x_prompt, ln_in_g, ln_in_b, params, ones_bd)
    y_sample = _trunk(x_sample, ln_in_g, ln_in_b, params, ones_bd)
    return (y_prompt, y_sample)
```
